```python
import math
import jax
import jax.numpy as jnp
from jax import lax
import numpy as np

D_MODEL = 2048
BATCH = 2
SEQ = 4096
DEPTH = 4
DEC_BATCH = 128
DEC_SEQ = 4
PAST_LEN = 8192
PAGE_SIZE = 128

MLA_HEADS = 16
MLA_NOPE = 64
MLA_ROPE = 32
MLA_V = 64
MLA_Q_LORA = 384
MLA_KV_LORA = 256
MLA_CACHE = MLA_KV_LORA + MLA_ROPE
MLA_SCALE = (MLA_NOPE + MLA_ROPE) ** -0.5
CONV_DIM = 1024
CONV_WIDTH = 31
NSA_HEADS = 16
NSA_DH = 64
NSA_ROT = NSA_DH // 4
NSA_BLK = 64
NSA_TOPN = 16
NSA_WINDOW = 512
NSA_PHI_HID = 64
NSA_SCALE = NSA_DH ** -0.5
N_EXPERTS = 32
TOP_K = 4
D_FF = 768
SWIGLU_LIMIT = 7.0
SWIGLU_ALPHA = 1.702
MOE_GROUP = 128
ROPE_THETA = 500000.0
Q_BLOCK = 128
DN_ALPHA = (2 * DEPTH) ** 0.25
DN_BETA = (8 * DEPTH) ** -0.25
NEG_INF = -1e30
FORCE_SCORE = 1e6
RMS_EPS = 1e-6
LN_EPS = 1e-5

IN_SIZES = (MLA_Q_LORA, MLA_KV_LORA, MLA_ROPE, 2 * CONV_DIM, NSA_HEADS * NSA_DH,
            2 * NSA_DH, 2 * NSA_DH, 2 * NSA_DH, 3 * NSA_HEADS, 3 * D_MODEL)
IN_SPLITS = tuple(int(s) for s in np.cumsum(IN_SIZES)[:-1])
N_IN = int(sum(IN_SIZES))

kernel_name = 'hybrid_mla_conformer_nsa_moe_step'


def rms_norm(x, g):
    xf = x.astype(jnp.float32)
    y = xf * lax.rsqrt(jnp.mean(xf * xf, axis=-1, keepdims=True) + RMS_EPS)
    return (y * g.astype(jnp.float32)).astype(x.dtype)


def layer_norm(x, g, b):
    xf = x.astype(jnp.float32)
    xc = xf - jnp.mean(xf, axis=-1, keepdims=True)
    var = jnp.mean(xc * xc, axis=-1, keepdims=True)
    y = xc * lax.rsqrt(var + LN_EPS) * g.astype(jnp.float32) + b.astype(jnp.float32)
    return y.astype(x.dtype)


def masked_softmax(s, mask):
    s = jnp.where(mask, s.astype(jnp.float32), NEG_INF)
    return jnp.where(mask, jax.nn.softmax(s, axis=-1), 0.0)


def apply_rope(x, pos, rot_dim):
    half = rot_dim // 2
    freqs = ROPE_THETA ** (-jnp.arange(half, dtype=jnp.float32) * (2.0 / rot_dim))
    ang = pos.astype(jnp.float32)[:, None] * freqs[None, :]
    shape = (1, ang.shape[0]) + (1,) * (x.ndim - 3) + (half,)
    cos = jnp.cos(ang).reshape(shape).astype(x.dtype)
    sin = jnp.sin(ang).reshape(shape).astype(x.dtype)
    x1, x2, rest = x[..., :half], x[..., half:rot_dim], x[..., rot_dim:]
    return jnp.concatenate([x1 * cos - x2 * sin, x2 * cos + x1 * sin, rest], axis=-1)


def gather_pages(pool, page_table, layer):
    g = pool[page_table, layer]
    return g.reshape(g.shape[0], -1, g.shape[-1])


def mixer_project(x, pos, lp):
    B, T, _ = x.shape
    h = jnp.einsum('btd,dn->btn', x, lp['w_in'])
    cq, ckv, kr, glu, qn, kv_cmp, kv_sel, kv_win, gn, gm = jnp.split(h, IN_SPLITS, axis=-1)
    cq = rms_norm(cq, lp['mla_q_norm'])
    q = jnp.einsum('btc,cn->btn', cq, lp['mla_w_uq']).reshape(B, T, MLA_HEADS, MLA_NOPE + MLA_ROPE)
    q_lat = jnp.einsum('bthn,chn->bthc', q[..., :MLA_NOPE], lp['mla_w_uk'])
    q_rope = apply_rope(q[..., MLA_NOPE:], pos, MLA_ROPE)
    mla_rows = jnp.concatenate([rms_norm(ckv, lp['mla_kv_norm']), apply_rope(kr, pos, MLA_ROPE)], axis=-1)
    u = glu[..., :CONV_DIM] * jax.nn.sigmoid(glu[..., CONV_DIM:])
    qn = qn.reshape(B, T, NSA_HEADS, NSA_DH)
    qr = apply_rope(qn, pos, NSA_ROT)
    kv_sel = jnp.concatenate([apply_rope(kv_sel[..., :NSA_DH], pos, NSA_ROT), kv_sel[..., NSA_DH:]], axis=-1)
    kv_win = jnp.concatenate([apply_rope(kv_win[..., :NSA_DH], pos, NSA_ROT), kv_win[..., NSA_DH:]], axis=-1)
    g_nsa = jax.nn.sigmoid(gn).reshape(B, T, NSA_HEADS, 3)
    return {'q_lat': q_lat, 'q_rope': q_rope, 'mla_rows': mla_rows, 'u': u, 'qn': qn, 'qr': qr,
            'kv_cmp': kv_cmp, 'kv_sel': kv_sel, 'kv_win': kv_win, 'g_nsa': g_nsa, 'gm': gm}


def mla_attend(q_lat, q_rope, q_pos, kv, k_pos, w_uv):
    c, kr = kv[..., :MLA_KV_LORA], kv[..., MLA_KV_LORA:]
    s = (jnp.einsum('bqhc,bkc->bhqk', q_lat, c) + jnp.einsum('bqhr,bkr->bhqk', q_rope, kr)) * MLA_SCALE
    p = masked_softmax(s, k_pos[None, :] <= q_pos[:, None]).astype(kv.dtype)
    o_lat = jnp.einsum('bhqk,bkc->bqhc', p, c)
    return jnp.einsum('bqhc,chv->bqhv', o_lat, w_uv)


def mla_prompt(q_lat, q_rope, kv, w_uv):
    B, S = q_lat.shape[:2]
    k_pos = jnp.arange(S)

    def q_block(i):
        st = i * Q_BLOCK
        ql = lax.dynamic_slice_in_dim(q_lat, st, Q_BLOCK, axis=1)
        qr = lax.dynamic_slice_in_dim(q_rope, st, Q_BLOCK, axis=1)
        return mla_attend(ql, qr, st + jnp.arange(Q_BLOCK), kv, k_pos, w_uv)

    o = lax.map(q_block, jnp.arange(S // Q_BLOCK))
    return o.transpose(1, 0, 2, 3, 4).reshape(B, S, MLA_HEADS * MLA_V)


def conv_branch(u_ext, lp):
    y = lax.conv_general_dilated(u_ext, lp['conv_w'][:, None, :], window_strides=(1,), padding='VALID',
                                 dimension_numbers=('NWC', 'WIO', 'NWC'), feature_group_count=CONV_DIM)
    y = layer_norm(y + lp['conv_b'], lp['conv_ln_g'], lp['conv_ln_b'])
    return jnp.einsum('btc,cd->btd', jax.nn.silu(y), lp['conv_w_br'])


def nsa_compress(kv_cmp, lp):
    B, L, _ = kv_cmp.shape
    nb = L // NSA_BLK
    blocks = kv_cmp[:, :nb * NSA_BLK].reshape(B, nb, NSA_BLK, 2, NSA_DH) + lp['nsa_phi_pe']
    flat = blocks.transpose(0, 1, 3, 2, 4).reshape(B, nb, 2, NSA_BLK * NSA_DH)
    hid = jax.nn.gelu(jnp.einsum('bnsf,sfh->bnsh', flat, lp['nsa_phi_w1']))
    out = jnp.einsum('bnsh,shd->bnsd', hid, lp['nsa_phi_w2'])
    return out[:, :, 0], out[:, :, 1]


def nsa_sel_blocks(kv_sel):
    B, L, _ = kv_sel.shape
    nbs = -(-L // NSA_BLK)
    kv = jnp.pad(kv_sel, ((0, 0), (0, nbs * NSA_BLK - L), (0, 0))).reshape(B, nbs, NSA_BLK, 2 * NSA_DH)
    return kv[..., :NSA_DH], kv[..., NSA_DH:]


def nsa_core(qn, qr, g, q_pos, kc, vc, ks_blk, vs_blk, kvw, kw_pos):
    B, Tq = qn.shape[:2]
    NB, NBS = kc.shape[1], ks_blk.shape[1]
    s_c = jnp.einsum('bqhd,bnd->bqhn', qn, kc) * NSA_SCALE
    c_mask = ((jnp.arange(NB) + 1) * NSA_BLK - 1)[None, :] <= q_pos[:, None]
    p_c = masked_softmax(s_c, c_mask[None, :, None, :])
    o_c = jnp.einsum('bqhn,bnd->bqhd', p_c.astype(vc.dtype), vc)
    imp = jnp.pad(p_c.sum(axis=2), ((0, 0), (0, 0), (0, NBS - NB)), constant_values=-1.0)
    blk = jnp.arange(NBS)[None, :]
    cur = (q_pos // NSA_BLK)[:, None]
    forced = (blk == 0) | (blk == cur) | (blk == cur - 1)
    score = jnp.where(blk > cur, -1.0, jnp.where(forced, FORCE_SCORE, imp))
    n_sel = min(NSA_TOPN, NBS)
    top_s, top_i = lax.top_k(score, n_sel)
    bidx = jnp.arange(B)[:, None, None]
    k_sel = ks_blk[bidx, top_i]
    v_sel = vs_blk[bidx, top_i].reshape(B, Tq, n_sel * NSA_BLK, NSA_DH)
    kpos = top_i[..., None] * NSA_BLK + jnp.arange(NSA_BLK)
    s_mask = (top_s >= 0.0)[..., None] & (kpos <= q_pos[None, :, None, None])
    s_s = (jnp.einsum('bqhd,bqnkd->bqhnk', qr, k_sel) * NSA_SCALE).reshape(B, Tq, NSA_HEADS, n_sel * NSA_BLK)
    p_s = masked_softmax(s_s, s_mask.reshape(B, Tq, 1, n_sel * NSA_BLK))
    o_s = jnp.einsum('bqhm,bqmd->bqhd', p_s.astype(v_sel.dtype), v_sel)
    kw, vw = kvw[..., :NSA_DH], kvw[..., NSA_DH:]
    s_w = jnp.einsum('bqhd,bkd->bqhk', qr, kw) * NSA_SCALE
    kp, qp = kw_pos[None, :], q_pos[:, None]
    w_mask = (kp <= qp) & (kp > qp - NSA_WINDOW) & (kp >= 0)
    p_w = masked_softmax(s_w, w_mask[None, :, None, :])
    o_w = jnp.einsum('bqhk,bkd->bqhd', p_w.astype(vw.dtype), vw)
    return g[..., 0:1] * o_c + g[..., 1:2] * o_s + g[..., 2:3] * o_w


def nsa_prompt(pr, lp):
    qn, qr, g = pr['qn'], pr['qr'], pr['g_nsa']
    B, S = qn.shape[:2]
    kc, vc = nsa_compress(pr['kv_cmp'], lp)
    ks_blk, vs_blk = nsa_sel_blocks(pr['kv_sel'])
    kvw_pad = jnp.pad(pr['kv_win'], ((0, 0), (NSA_WINDOW, 0), (0, 0)))

    def q_block(i):
        st = i * Q_BLOCK
        take = lambda a: lax.dynamic_slice_in_dim(a, st, Q_BLOCK, axis=1)
        kvw = lax.dynamic_slice_in_dim(kvw_pad, st, NSA_WINDOW + Q_BLOCK, axis=1)
        kw_pos = st - NSA_WINDOW + jnp.arange(NSA_WINDOW + Q_BLOCK)
        return nsa_core(take(qn), take(qr), take(g), st + jnp.arange(Q_BLOCK), kc, vc, ks_blk, vs_blk, kvw, kw_pos)

    o = lax.map(q_block, jnp.arange(S // Q_BLOCK))
    return o.transpose(1, 0, 2, 3, 4).reshape(B, S, NSA_HEADS * NSA_DH)


def mixer_out(o_mla, y_conv, o_nsa, gm, lp):
    B, T, _ = gm.shape
    y_a = jnp.einsum('btn,nd->btd', o_mla, lp['mla_w_br'])
    y_n = jnp.einsum('btn,nd->btd', o_nsa, lp['nsa_w_br'])
    g = jax.nn.sigmoid(gm).reshape(B, T, 3, D_MODEL)
    m = g[:, :, 0] * y_a + g[:, :, 1] * y_conv + g[:, :, 2] * y_n
    return jnp.einsum('btd,de->bte', m, lp['w_out'])


def moe(x, router_w, router_b, w_gu, b_gu, w_dn, b_dn):
    B, T, D = x.shape
    xf = x.reshape(B * T, D)
    n_asg = B * T * TOP_K
    logits = jnp.einsum('td,de->te', xf, router_w) + router_b
    top_v, top_e = lax.top_k(logits, TOP_K)
    gates = jax.nn.softmax(top_v.astype(jnp.float32), axis=-1).astype(x.dtype)
    flat_e = top_e.reshape(-1)
    order = jnp.argsort(flat_e)
    e_sorted = flat_e[order]
    tok_sorted = (order // TOP_K).astype(jnp.int32)
    w_sorted = gates.reshape(-1)[order]
    counts = jnp.bincount(flat_e, length=N_EXPERTS)
    padded = (counts + MOE_GROUP - 1) // MOE_GROUP * MOE_GROUP
    pad_end = jnp.cumsum(padded)
    pad_start = pad_end - padded
    grp_start = jnp.cumsum(counts) - counts
    dest = pad_start[e_sorted] + jnp.arange(n_asg) - grp_start[e_sorted]
    n_blocks = -(-n_asg // MOE_GROUP) + N_EXPERTS
    n_slots = n_blocks * MOE_GROUP
    slot_tok = jnp.zeros((n_slots,), jnp.int32).at[dest].set(tok_sorted)
    slot_w = jnp.zeros((n_slots,), x.dtype).at[dest].set(w_sorted)
    blk_e = jnp.minimum(jnp.searchsorted(pad_end, jnp.arange(n_blocks) * MOE_GROUP, side='right'), N_EXPERTS - 1)
    xs = xf[slot_tok].reshape(n_blocks, MOE_GROUP, D)

    def expert_block(args):
        xb, e = args
        h = xb @ w_gu[e] + b_gu[e]
        gate = jnp.minimum(h[:, :D_FF], SWIGLU_LIMIT)
        up = jnp.clip(h[:, D_FF:], -SWIGLU_LIMIT, SWIGLU_LIMIT)
        act = gate * jax.nn.sigmoid(SWIGLU_ALPHA * gate) * (up + 1.0)
        return act @ w_dn[e] + b_dn[e]

    ys = lax.map(expert_block, (xs, blk_e)).reshape(n_slots, D)
    out = jnp.zeros_like(xf).at[slot_tok].add(ys * slot_w[:, None])
    return out.reshape(B, T, D)


def residual_update(x, mix, lp):
    x = layer_norm(DN_ALPHA * x + mix, lp['ln1_g'], lp['ln1_b'])
    f = moe(x, lp['router_w'], lp['router_b'], lp['moe_w_gu'], lp['moe_b_gu'], lp['moe_w_dn'], lp['moe_b_dn'])
    return layer_norm(DN_ALPHA * x + f, lp['ln2_g'], lp['ln2_b'])


def prompt_layer(x, lp):
    B, S, _ = x.shape
    pr = mixer_project(x, jnp.arange(S), lp)
    o_mla = mla_prompt(pr['q_lat'], pr['q_rope'], pr['mla_rows'], lp['mla_w_uv'])
    y_conv = conv_branch(jnp.pad(pr['u'], ((0, 0), (CONV_WIDTH - 1, 0), (0, 0))), lp)
    o_nsa = nsa_prompt(pr, lp)
    y = residual_update(x, mixer_out(o_mla, y_conv, o_nsa, pr['gm'], lp), lp)
    new = (pr['mla_rows'], pr['kv_cmp'], pr['kv_sel'], pr['kv_win'][:, -min(NSA_WINDOW, S):],
           pr['u'][:, -(CONV_WIDTH - 1):])
    return y, new


def sample_layer(x, lp, mla_past, cmp_past, sel_past, win_buf, conv_buf):
    B, T, _ = x.shape
    P = mla_past.shape[1]
    pos = P + jnp.arange(T)
    pr = mixer_project(x, pos, lp)
    kv_all = jnp.concatenate([mla_past, pr['mla_rows']], axis=1)
    o_mla = mla_attend(pr['q_lat'], pr['q_rope'], pos, kv_all, jnp.arange(P + T), lp['mla_w_uv'])
    o_mla = o_mla.reshape(B, T, MLA_HEADS * MLA_V)
    u_ext = jnp.concatenate([conv_buf, pr['u']], axis=1)
    y_conv = conv_branch(u_ext, lp)
    kc, vc = nsa_compress(jnp.concatenate([cmp_past, pr['kv_cmp']], axis=1), lp)
    ks_blk, vs_blk = nsa_sel_blocks(jnp.concatenate([sel_past, pr['kv_sel']], axis=1))
    wb = win_buf.shape[1]
    kvw = jnp.concatenate([win_buf, pr['kv_win']], axis=1)
    kw_pos = P - wb + jnp.arange(wb + T)
    o_nsa = nsa_core(pr['qn'], pr['qr'], pr['g_nsa'], pos, kc, vc, ks_blk, vs_blk, kvw, kw_pos)
    o_nsa = o_nsa.reshape(B, T, NSA_HEADS * NSA_DH)
    y = residual_update(x, mixer_out(o_mla, y_conv, o_nsa, pr['gm'], lp), lp)
    new = (pr['mla_rows'], pr['kv_cmp'], pr['kv_sel'], kvw[:, -wb:], u_ext[:, -(CONV_WIDTH - 1):])
    return y, new


def setup_inputs(seed: int = 0) -> dict:
    key = jax.random.key(seed)
    ks = jax.random.split(key, 40)
    f32 = jnp.float32

    def nrm(k, shape, scale):
        return jax.random.normal(k, shape, f32) * scale

    n_pages = PAST_LEN // PAGE_SIZE
    n_pool = (DEC_BATCH * n_pages * 5) // 4
    win = min(NSA_WINDOW, PAST_LEN)
    page_table = jax.random.permutation(ks[0], n_pool)[: DEC_BATCH * n_pages].reshape(DEC_BATCH, n_pages).astype(jnp.int32)
    return {
        'x_prompt': nrm(ks[1], (BATCH, SEQ, D_MODEL), 1.0),
        'x_sample': nrm(ks[2], (DEC_BATCH, DEC_SEQ, D_MODEL), 1.0),
        'cache_mla': nrm(ks[3], (n_pool, DEPTH, PAGE_SIZE, MLA_CACHE), 1.0),
        'cache_nsa_cmp': nrm(ks[4], (n_pool, DEPTH, PAGE_SIZE, 2 * NSA_DH), 1.0),
        'cache_nsa_sel': nrm(ks[5], (n_pool, DEPTH, PAGE_SIZE, 2 * NSA_DH), 1.0),
        'state_nsa_win': nrm(ks[6], (DEC_BATCH, DEPTH, win, 2 * NSA_DH), 1.0),
        'state_conv': nrm(ks[7], (DEC_BATCH, DEPTH, CONV_WIDTH - 1, CONV_DIM), 1.0),
        'page_table': page_table,
        'w_in': nrm(ks[8], (DEPTH, D_MODEL, N_IN), D_MODEL ** -0.5),
        'mla_q_norm': 1.0 + nrm(ks[9], (DEPTH, MLA_Q_LORA), 0.01),
        'mla_kv_norm': 1.0 + nrm(ks[10], (DEPTH, MLA_KV_LORA), 0.01),
        'mla_w_uq': nrm(ks[11], (DEPTH, MLA_Q_LORA, MLA_HEADS * (MLA_NOPE + MLA_ROPE)), MLA_Q_LORA ** -0.5),
        'mla_w_uk': nrm(ks[12], (DEPTH, MLA_KV_LORA, MLA_HEADS, MLA_NOPE), MLA_KV_LORA ** -0.5),
        'mla_w_uv': nrm(ks[13], (DEPTH, MLA_KV_LORA, MLA_HEADS, MLA_V), MLA_KV_LORA ** -0.5),
        'mla_w_br': nrm(ks[14], (DEPTH, MLA_HEADS * MLA_V, D_MODEL), DN_BETA * (MLA_HEADS * MLA_V) ** -0.5),
        'conv_w': nrm(ks[15], (DEPTH, CONV_WIDTH, CONV_DIM), CONV_WIDTH ** -0.5),
        'conv_b': nrm(ks[16], (DEPTH, CONV_DIM), 0.01),
        'conv_ln_g': 1.0 + nrm(ks[17], (DEPTH, CONV_DIM), 0.01),
        'conv_ln_b': nrm(ks[18], (DEPTH, CONV_DIM), 0.01),
        'conv_w_br': nrm(ks[19], (DEPTH, CONV_DIM, D_MODEL), DN_BETA * CONV_DIM ** -0.5),
        'nsa_phi_pe': nrm(ks[20], (DEPTH, NSA_BLK, 2, NSA_DH), 0.1),
        'nsa_phi_w1': nrm(ks[21], (DEPTH, 2, NSA_BLK * NSA_DH, NSA_PHI_HID), (NSA_BLK * NSA_DH) ** -0.5),
        'nsa_phi_w2': nrm(ks[22], (DEPTH, 2, NSA_PHI_HID, NSA_DH), NSA_PHI_HID ** -0.5),
        'nsa_w_br': nrm(ks[23], (DEPTH, NSA_HEADS * NSA_DH, D_MODEL), DN_BETA * (NSA_HEADS * NSA_DH) ** -0.5),
        'w_out': nrm(ks[24], (DEPTH, D_MODEL, D_MODEL), DN_BETA * D_MODEL ** -0.5),
        'ln1_g': 1.0 + nrm(ks[25], (DEPTH, D_MODEL), 0.01),
        'ln1_b': nrm(ks[26], (DEPTH, D_MODEL), 0.01),
        'ln2_g': 1.0 + nrm(ks[27], (DEPTH, D_MODEL), 0.01),
        'ln2_b': nrm(ks[28], (DEPTH, D_MODEL), 0.01),
        'router_w': nrm(ks[29], (DEPTH, D_MODEL, N_EXPERTS), D_MODEL ** -0.5),
        'router_b': nrm(ks[30], (DEPTH, N_EXPERTS), 0.01),
        'moe_w_gu': nrm(ks[31], (DEPTH, N_EXPERTS, D_MODEL, 2 * D_FF), D_MODEL ** -0.5),
        'moe_b_gu': nrm(ks[32], (DEPTH, N_EXPERTS, 2 * D_FF), 0.01),
        'moe_w_dn': nrm(ks[33], (DEPTH, N_EXPERTS, D_FF, D_MODEL), DN_BETA * D_FF ** -0.5),
        'moe_b_dn': nrm(ks[34], (DEPTH, N_EXPERTS, D_MODEL), 0.01),
    }


def reference(x_prompt, x_sample, cache_mla, cache_nsa_cmp, cache_nsa_sel, state_nsa_win, state_conv, page_table,
              w_in, mla_q_norm, mla_kv_norm, mla_w_uq, mla_w_uk, mla_w_uv, mla_w_br,
              conv_w, conv_b, conv_ln_g, conv_ln_b, conv_w_br,
              nsa_phi_pe, nsa_phi_w1, nsa_phi_w2, nsa_w_br, w_out,
              ln1_g, ln1_b, ln2_g, ln2_b, router_w, router_b, moe_w_gu, moe_b_gu, moe_w_dn, moe_b_dn):
    xp, xs = x_prompt, x_sample
    st_p, st_s = [], []
    for l in range(DEPTH):
        lp = {'w_in': w_in[l], 'mla_q_norm': mla_q_norm[l], 'mla_kv_norm': mla_kv_norm[l],
              'mla_w_uq': mla_w_uq[l], 'mla_w_uk': mla_w_uk[l], 'mla_w_uv': mla_w_uv[l], 'mla_w_br': mla_w_br[l],
              'conv_w': conv_w[l], 'conv_b': conv_b[l], 'conv_ln_g': conv_ln_g[l], 'conv_ln_b': conv_ln_b[l],
              'conv_w_br': conv_w_br[l], 'nsa_phi_pe': nsa_phi_pe[l], 'nsa_phi_w1': nsa_phi_w1[l],
              'nsa_phi_w2': nsa_phi_w2[l], 'nsa_w_br': nsa_w_br[l], 'w_out': w_out[l],
              'ln1_g': ln1_g[l], 'ln1_b': ln1_b[l], 'ln2_g': ln2_g[l], 'ln2_b': ln2_b[l],
              'router_w': router_w[l], 'router_b': router_b[l], 'moe_w_gu': moe_w_gu[l], 'moe_b_gu': moe_b_gu[l],
              'moe_w_dn': moe_w_dn[l], 'moe_b_dn': moe_b_dn[l]}
        xp, sp = prompt_layer(xp, lp)
        xs, ss = sample_layer(xs, lp,
                              gather_pages(cache_mla, page_table, l),
                              gather_pages(cache_nsa_cmp, page_table, l),
                              gather_pages(cache_nsa_sel, page_table, l),
                              state_nsa_win[:, l], state_conv[:, l])
        st_p.append(sp)
        st_s.append(ss)
    stack = lambda sts, i: jnp.stack([s[i] for s in sts], axis=1)
    new_mla_p, new_cmp_p, new_sel_p = stack(st_p, 0), stack(st_p, 1), stack(st_p, 2)
    new_win_p, new_conv_p = stack(st_p, 3), stack(st_p, 4)
    new_mla_s, new_cmp_s, new_sel_s = stack(st_s, 0), stack(st_s, 1), stack(st_s, 2)
    new_win_s, new_conv_s = stack(st_s, 3), stack(st_s, 4)
    return (xp, xs, new_mla_p, new_cmp_p, new_sel_p, new_win_p, new_conv_p,
            new_mla_s, new_cmp_s, new_sel_s, new_win_s, new_conv_s)
```

```python
import functools

import numpy as np
import jax
import jax.numpy as jnp
from jax import lax
from jax.experimental import pallas as pl
from jax.experimental.pallas import tpu as pltpu

F32 = jnp.float32
BF16 = jnp.bfloat16

D_MODEL = 2048
BATCH = 2
SEQ = 4096
DEPTH = 4
DEC_BATCH = 128
DEC_SEQ = 4
PAST_LEN = 8192
PAGE_SIZE = 128
N_PAGES = PAST_LEN // PAGE_SIZE

MLA_HEADS = 16
MLA_NOPE = 64
MLA_ROPE = 32
MLA_V = 64
MLA_Q_LORA = 384
MLA_KV_LORA = 256
MLA_CACHE = MLA_KV_LORA + MLA_ROPE
MLA_SCALE = (MLA_NOPE + MLA_ROPE) ** -0.5
CONV_DIM = 1024
CONV_WIDTH = 31
NSA_HEADS = 16
NSA_DH = 64
NSA_ROT = NSA_DH // 4
NSA_BLK = 64
NSA_TOPN = 16
NSA_WINDOW = 512
NSA_PHI_HID = 64
NSA_SCALE = NSA_DH ** -0.5
N_EXPERTS = 32
TOP_K = 4
D_FF = 768
SWIGLU_LIMIT = 7.0
SWIGLU_ALPHA = 1.702
ROPE_THETA = 500000.0
Q_BLOCK = 128
DN_ALPHA = (2 * DEPTH) ** 0.25
NEG_INF = -1e30
FORCE_SCORE = 1e6
RMS_EPS = 1e-6
LN_EPS = 1e-5

M_PROMPT = BATCH * SEQ
M_SAMPLE = DEC_BATCH * DEC_SEQ
M_ALL = M_PROMPT + M_SAMPLE

V7X_VMEM_LIMIT_CAP = 60000 * 1024
LANE = 128

IN_CQ = 0
IN_CKV = 384
IN_KR = 640
IN_GLU = 768
IN_QN = 2816
IN_CMP = 3840
IN_SEL = 3968
IN_WIN = 4096
IN_GN = 4224
IN_GM = 4352
IN_PAD_N = 10752

MOE_TM = 256
MOE_BLOCKS = (M_ALL * TOP_K) // MOE_TM + N_EXPERTS
MOE_SLOTS = MOE_BLOCKS * MOE_TM

CMP_PITCH = 72
HEAD_SHIFT = 4
BLK_SHIFT = 6
assert (1 << HEAD_SHIFT) == MLA_HEADS == NSA_HEADS and (1 << BLK_SHIFT) == NSA_BLK


def _cparams(sem, est_bytes):
    limit = int(min(max(est_bytes * 5 // 4 + (4 << 20), 32 << 20), V7X_VMEM_LIMIT_CAP))
    return pltpu.CompilerParams(dimension_semantics=sem, vmem_limit_bytes=limit)


def _dot(a, b):
    return jnp.dot(a, b, preferred_element_type=F32)


def _dot_nt(a, b):
    return lax.dot_general(a, b, (((1,), (1,)), ((), ())), preferred_element_type=F32)


def _mm_kernel(x_ref, w_ref, o_ref):
    o_ref[...] = _dot(x_ref[...].astype(BF16), w_ref[...].astype(BF16)).astype(o_ref.dtype)


def _mm(x, w, *, tm, tn, out_dtype=F32, name):
    m, k = x.shape
    n = w.shape[1]
    est = 2 * (tm * k * x.dtype.itemsize + k * tn * w.dtype.itemsize + tm * tn * 4) + tm * k * 2 + k * tn * 2
    return pl.pallas_call(
        _mm_kernel,
        out_shape=jax.ShapeDtypeStruct((m, n), out_dtype),
        grid=(pl.cdiv(n, tn), pl.cdiv(m, tm)),
        in_specs=[pl.BlockSpec((tm, k), lambda j, i: (i, 0)),
                  pl.BlockSpec((k, tn), lambda j, i: (0, j))],
        out_specs=pl.BlockSpec((tm, tn), lambda j, i: (i, j)),
        compiler_params=_cparams(("arbitrary", "arbitrary"), est),
        name=name,
    )(x, w)


def _mm_f32_kernel(x_ref, w_ref, o_ref):
    o_ref[...] = jnp.dot(x_ref[...], w_ref[...], preferred_element_type=F32,
                         precision=lax.Precision.HIGHEST)


def _mm_f32(x, w, *, tm, name):
    m, k = x.shape
    n = w.shape[1]
    est = 2 * (tm * k * 4 + k * LANE * 4 + tm * LANE * 4)
    return pl.pallas_call(
        _mm_f32_kernel,
        out_shape=jax.ShapeDtypeStruct((m, n), F32),
        grid=(pl.cdiv(m, tm),),
        in_specs=[pl.BlockSpec((tm, k), lambda i: (i, 0)),
                  pl.BlockSpec((k, n), lambda i: (0, 0))],
        out_specs=pl.BlockSpec((tm, n), lambda i: (i, 0)),
        compiler_params=_cparams(("arbitrary",), est),
        name=name,
    )(x, w)


def _head_up_kernel(x_ref, w_ref, o_ref):
    kh = w_ref.shape[1]
    x = x_ref[...].astype(BF16)
    for j in range(2):
        o_ref[j] = _dot(x[:, j * kh:(j + 1) * kh], w_ref[j]).astype(o_ref.dtype)


def _head_up(x, w, *, tm, out_dtype, name):
    m = x.shape[0]
    nheads, kh, nh = w.shape
    est = 2 * (tm * 2 * kh * 4 + 2 * kh * nh * 2 + 2 * tm * nh * 4)
    return pl.pallas_call(
        _head_up_kernel,
        out_shape=jax.ShapeDtypeStruct((nheads, m, nh), out_dtype),
        grid=(nheads // 2, pl.cdiv(m, tm)),
        in_specs=[pl.BlockSpec((tm, 2 * kh), lambda h, i: (i, h)),
                  pl.BlockSpec((2, kh, nh), lambda h, i: (h, 0, 0))],
        out_specs=pl.BlockSpec((2, tm, nh), lambda h, i: (h, i, 0)),
        compiler_params=_cparams(("arbitrary", "arbitrary"), est),
        name=name,
    )(x, w)


def _head_down_kernel(x_ref, w_ref, o_ref):
    o_ref[...] = jnp.concatenate(
        [_dot(x_ref[j].astype(BF16), w_ref[j]) for j in range(2)], axis=1).astype(o_ref.dtype)


def _head_down(x, w, *, tm, out_dtype, name):
    nheads, m, kh = x.shape
    nh = w.shape[2]
    est = 2 * (2 * tm * kh * 4 + 2 * kh * nh * 2 + tm * 2 * nh * 4)
    return pl.pallas_call(
        _head_down_kernel,
        out_shape=jax.ShapeDtypeStruct((m, nheads * nh), out_dtype),
        grid=(nheads // 2, pl.cdiv(m, tm)),
        in_specs=[pl.BlockSpec((2, tm, kh), lambda h, i: (h, i, 0)),
                  pl.BlockSpec((2, kh, nh), lambda h, i: (h, 0, 0))],
        out_specs=pl.BlockSpec((tm, 2 * nh), lambda h, i: (i, h)),
        compiler_params=_cparams(("arbitrary", "arbitrary"), est),
        name=name,
    )(x, w)


def _softmax_init(m_sc, l_sc, acc_sc):
    m_sc[...] = jnp.full(m_sc.shape, NEG_INF, F32)
    l_sc[...] = jnp.zeros(l_sc.shape, F32)
    acc_sc[...] = jnp.zeros(acc_sc.shape, F32)


def _softmax_update(s, v, m_sc, l_sc, acc_sc):
    m_prev = m_sc[...]
    m_new = jnp.maximum(m_prev, jnp.max(s, axis=1, keepdims=True))
    alpha = jnp.exp(m_prev - m_new)
    p = jnp.exp(s - m_new)
    l_sc[...] = alpha * l_sc[...] + jnp.sum(p, axis=1, keepdims=True)
    acc_sc[...] = alpha * acc_sc[...] + _dot(p.astype(BF16), v)
    m_sc[...] = m_new


def _mla_prompt_kernel(ql_ref, qr_ref, kv_ref, o_ref, m_sc, l_sc, acc_sc, *, tk):
    qi = pl.program_id(1)
    kb = pl.program_id(2)
    rows = MLA_HEADS * Q_BLOCK
    last = ((qi + 1) * Q_BLOCK - 1) // tk

    @pl.when(kb == 0)
    def _():
        _softmax_init(m_sc, l_sc, acc_sc)

    @pl.when(kb <= last)
    def _():
        ql = ql_ref[...].reshape(rows, MLA_KV_LORA)
        qr = qr_ref[...].reshape(rows, MLA_ROPE)
        kv = kv_ref[...]
        c = kv[:, :MLA_KV_LORA].astype(BF16)
        kr = kv[:, MLA_KV_LORA:].astype(BF16)
        s = (_dot_nt(ql, c) + _dot_nt(qr, kr)) * MLA_SCALE
        qpos = qi * Q_BLOCK + (lax.broadcasted_iota(jnp.int32, (rows, tk), 0) & (Q_BLOCK - 1))
        kpos = kb * tk + lax.broadcasted_iota(jnp.int32, (rows, tk), 1)
        s = jnp.where(kpos <= qpos, s, NEG_INF)
        _softmax_update(s, c, m_sc, l_sc, acc_sc)

    @pl.when(kb == pl.num_programs(2) - 1)
    def _():
        o = acc_sc[...] / l_sc[...]
        o_ref[...] = o.reshape(MLA_HEADS, Q_BLOCK, MLA_KV_LORA).astype(o_ref.dtype)


def _mla_prompt(ql, qr, kv, *, tk=512):
    nq = SEQ // Q_BLOCK
    nk = SEQ // tk
    rows = MLA_HEADS * Q_BLOCK

    def kv_map(b, qi, kb):
        return (b, jnp.minimum(kb, ((qi + 1) * Q_BLOCK - 1) // tk), 0)

    est = (2 * (rows * MLA_KV_LORA * 2 + rows * LANE * 2 + tk * 3 * LANE * 4 + rows * MLA_KV_LORA * 2)
           + rows * (MLA_KV_LORA + 2 * LANE) * 4 + 4 * rows * tk * 4)
    return pl.pallas_call(
        functools.partial(_mla_prompt_kernel, tk=tk),
        out_shape=jax.ShapeDtypeStruct((MLA_HEADS, BATCH, SEQ, MLA_KV_LORA), BF16),
        grid=(BATCH, nq, nk),
        in_specs=[pl.BlockSpec((MLA_HEADS, None, Q_BLOCK, MLA_KV_LORA), lambda b, qi, kb: (0, b, qi, 0)),
                  pl.BlockSpec((MLA_HEADS, None, Q_BLOCK, MLA_ROPE), lambda b, qi, kb: (0, b, qi, 0)),
                  pl.BlockSpec((None, tk, MLA_CACHE), kv_map)],
        out_specs=pl.BlockSpec((MLA_HEADS, None, Q_BLOCK, MLA_KV_LORA), lambda b, qi, kb: (0, b, qi, 0)),
        scratch_shapes=[pltpu.VMEM((rows, 1), F32), pltpu.VMEM((rows, 1), F32),
                        pltpu.VMEM((rows, MLA_KV_LORA), F32)],
        compiler_params=_cparams(("arbitrary", "arbitrary", "arbitrary"), est),
        name="mla_prompt",
    )(ql, qr, kv)


def _mla_sample_kernel(pt_ref, ql_ref, qr_ref, new_ref, *rest, pp):
    del pt_ref
    pages = rest[:pp]
    o_ref, m_sc, l_sc, acc_sc = rest[pp:]
    ch = pl.program_id(1)
    rows = DEC_SEQ * MLA_HEADS

    @pl.when(ch == 0)
    def _():
        _softmax_init(m_sc, l_sc, acc_sc)

    ql = ql_ref[...]
    qr = qr_ref[...]
    kv = jnp.concatenate([p[...] for p in pages], axis=0)
    c = kv[:, :MLA_KV_LORA].astype(BF16)
    kr = kv[:, MLA_KV_LORA:].astype(BF16)
    s = (_dot_nt(ql, c) + _dot_nt(qr, kr)) * MLA_SCALE
    _softmax_update(s, c, m_sc, l_sc, acc_sc)

    @pl.when(ch == pl.num_programs(1) - 1)
    def _():
        kvn = new_ref[...]
        cn = kvn[:, :MLA_KV_LORA].astype(BF16)
        krn = kvn[:, MLA_KV_LORA:].astype(BF16)
        sn = (_dot_nt(ql, cn) + _dot_nt(qr, krn)) * MLA_SCALE
        t = lax.broadcasted_iota(jnp.int32, (rows, PAGE_SIZE), 0) >> HEAD_SHIFT
        j = lax.broadcasted_iota(jnp.int32, (rows, PAGE_SIZE), 1)
        sn = jnp.where(j <= t, sn, NEG_INF)
        _softmax_update(sn, cn, m_sc, l_sc, acc_sc)
        o_ref[...] = (acc_sc[...] / l_sc[...]).astype(o_ref.dtype)


def _mla_sample(page_table, ql, qr, new_rows, cache, layer, *, pp=16):
    rows = DEC_SEQ * MLA_HEADS
    nch = N_PAGES // pp

    def page_spec(p):
        return pl.BlockSpec((None, None, PAGE_SIZE, MLA_CACHE),
                            lambda b, ch, pt: (pt[b, ch * pp + p], layer, 0, 0))

    est = (2 * (pp + 1) * PAGE_SIZE * 3 * LANE * 4 + 3 * pp * PAGE_SIZE * 3 * LANE * 4
           + 6 * rows * pp * PAGE_SIZE * 4)
    grid_spec = pltpu.PrefetchScalarGridSpec(
        num_scalar_prefetch=1,
        grid=(DEC_BATCH, nch),
        in_specs=[pl.BlockSpec((None, rows, MLA_KV_LORA), lambda b, ch, pt: (b, 0, 0)),
                  pl.BlockSpec((None, rows, MLA_ROPE), lambda b, ch, pt: (b, 0, 0)),
                  pl.BlockSpec((None, PAGE_SIZE, MLA_CACHE), lambda b, ch, pt: (b, 0, 0))]
        + [page_spec(p) for p in range(pp)],
        out_specs=pl.BlockSpec((None, rows, MLA_KV_LORA), lambda b, ch, pt: (b, 0, 0)),
        scratch_shapes=[pltpu.VMEM((rows, 1), F32), pltpu.VMEM((rows, 1), F32),
                        pltpu.VMEM((rows, MLA_KV_LORA), F32)],
    )
    return pl.pallas_call(
        functools.partial(_mla_sample_kernel, pp=pp),
        out_shape=jax.ShapeDtypeStruct((DEC_BATCH, rows, MLA_KV_LORA), BF16),
        grid_spec=grid_spec,
        compiler_params=_cparams(("arbitrary", "arbitrary"), est),
        name="mla_sample",
    )(page_table, ql, qr, new_rows, *([cache] * pp))


def _conv_prompt_kernel(prev_ref, cur_ref, w_ref, b_ref, g_ref, beta_ref, o_ref, ext_sc, *, tm):
    i = pl.program_id(1)
    halo = 32
    sub = 32
    prev = prev_ref[...]
    ext_sc[0:halo, :] = jnp.where(i > 0, prev, 0.0)
    ext_sc[halo:halo + tm, :] = cur_ref[...]
    for r0 in range(0, tm, sub):
        acc = jnp.zeros((sub, CONV_DIM), F32)
        for k in range(CONV_WIDTH):
            acc = acc + ext_sc[pl.ds(r0 + halo - (CONV_WIDTH - 1) + k, sub), :] * w_ref[k:k + 1, :]
        y = acc + b_ref[...]
        mu = jnp.mean(y, axis=1, keepdims=True)
        yc = y - mu
        var = jnp.mean(yc * yc, axis=1, keepdims=True)
        z = yc * lax.rsqrt(var + LN_EPS) * g_ref[...] + beta_ref[...]
        o_ref[r0:r0 + sub, :] = (z * jax.nn.sigmoid(z)).astype(o_ref.dtype)


def _conv_prompt(u, w, b, g, beta, *, tm=256):
    halo = 32
    est = 2 * (halo + 2 * tm) * CONV_DIM * 4 + (halo + 6 * tm) * CONV_DIM * 4
    vec = pl.BlockSpec((1, CONV_DIM), lambda bb, i: (0, 0))
    return pl.pallas_call(
        functools.partial(_conv_prompt_kernel, tm=tm),
        out_shape=jax.ShapeDtypeStruct((BATCH, SEQ, CONV_DIM), BF16),
        grid=(BATCH, SEQ // tm),
        in_specs=[pl.BlockSpec((None, halo, CONV_DIM),
                               lambda bb, i: (bb, jnp.maximum(i * (tm // halo) - 1, 0), 0)),
                  pl.BlockSpec((None, tm, CONV_DIM), lambda bb, i: (bb, i, 0)),
                  pl.BlockSpec((CONV_WIDTH, CONV_DIM), lambda bb, i: (0, 0)),
                  vec, vec, vec],
        out_specs=pl.BlockSpec((None, tm, CONV_DIM), lambda bb, i: (bb, i, 0)),
        scratch_shapes=[pltpu.VMEM((halo + tm, CONV_DIM), F32)],
        compiler_params=_cparams(("arbitrary", "arbitrary"), est),
        name="conv_prompt",
    )(u, u, w, b.reshape(1, -1), g.reshape(1, -1), beta.reshape(1, -1))


def _masked_softmax_rows(s, mask):
    s = jnp.where(mask, s, NEG_INF)
    m = jnp.max(s, axis=1, keepdims=True)
    p = jnp.where(mask, jnp.exp(s - m), 0.0)
    l = jnp.sum(p, axis=1, keepdims=True)
    return p / jnp.where(l > 0.0, l, 1.0)


def _nsa_cmp_prompt_kernel(q_ref, kcvc_ref, oc_ref, imp_ref, *, tq):
    i = pl.program_id(1)
    nb = kcvc_ref.shape[0]
    kcvc = kcvc_ref[...]
    kc = kcvc[:, :NSA_DH].astype(BF16)
    vc = kcvc[:, NSA_DH:].astype(BF16)
    qpos = i * tq + lax.broadcasted_iota(jnp.int32, (tq, nb), 0)
    bend = (lax.broadcasted_iota(jnp.int32, (tq, nb), 1) + 1) * NSA_BLK - 1
    mask = bend <= qpos
    q = q_ref[...].astype(BF16)
    imp = jnp.zeros((tq, nb), F32)
    outs = []
    for h in range(NSA_HEADS):
        s = _dot_nt(q[:, h * NSA_DH:(h + 1) * NSA_DH], kc) * NSA_SCALE
        p = _masked_softmax_rows(s, mask)
        imp = imp + p
        outs.append(_dot(p.astype(BF16), vc))
    oc_ref[...] = jnp.concatenate(outs, axis=1)
    imp_ref[...] = imp


def _nsa_cmp_prompt(qn, kcvc, *, tq=256):
    nb = kcvc.shape[1]
    hd = NSA_HEADS * NSA_DH
    est = 2 * (tq * hd * 4 * 2 + nb * LANE * 4 + tq * LANE * 4) + 8 * tq * hd * 4
    return pl.pallas_call(
        functools.partial(_nsa_cmp_prompt_kernel, tq=tq),
        out_shape=(jax.ShapeDtypeStruct((BATCH, SEQ, hd), F32),
                   jax.ShapeDtypeStruct((BATCH, SEQ, nb), F32)),
        grid=(BATCH, SEQ // tq),
        in_specs=[pl.BlockSpec((None, tq, hd), lambda b, i: (b, i, 0)),
                  pl.BlockSpec((None, nb, 2 * NSA_DH), lambda b, i: (b, 0, 0))],
        out_specs=(pl.BlockSpec((None, tq, hd), lambda b, i: (b, i, 0)),
                   pl.BlockSpec((None, tq, nb), lambda b, i: (b, i, 0))),
        compiler_params=_cparams(("arbitrary", "arbitrary"), est),
        name="nsa_cmp_prompt",
    )(qn, kcvc)


def _nsa_cmp_sample_kernel(pt_ref, q_ref, pe_ref, w1_ref, w2_ref, *rest):
    del pt_ref
    pages = rest[:N_PAGES]
    oc_ref, imp_ref, x_sc = rest[N_PAGES:]
    nb = 2 * N_PAGES
    for p in range(N_PAGES):
        pg = pages[p][...]
        x_sc[pl.ds((2 * p) * CMP_PITCH, NSA_BLK), :] = pg[:NSA_BLK]
        x_sc[pl.ds((2 * p + 1) * CMP_PITCH, NSA_BLK), :] = pg[NSA_BLK:]
    acc = jnp.zeros((nb, 2 * NSA_PHI_HID), F32)
    for r2 in range(NSA_BLK // 2):
        a0 = x_sc[pl.ds(2 * r2, nb, stride=CMP_PITCH), :] + pe_ref[2 * r2:2 * r2 + 1, :]
        a1 = x_sc[pl.ds(2 * r2 + 1, nb, stride=CMP_PITCH), :] + pe_ref[2 * r2 + 1:2 * r2 + 2, :]
        a = jnp.concatenate([a0, a1], axis=1).astype(BF16)
        acc = acc + _dot(a, w1_ref[r2])
    hid = jax.nn.gelu(acc)
    kcvc = _dot(hid.astype(BF16), w2_ref[...])
    kc = kcvc[:, :NSA_DH].astype(BF16)
    vc = kcvc[:, NSA_DH:].astype(BF16)
    rows = DEC_SEQ * NSA_HEADS
    s = _dot_nt(q_ref[...], kc) * NSA_SCALE
    qpos = PAST_LEN + (lax.broadcasted_iota(jnp.int32, (rows, nb), 0) >> HEAD_SHIFT)
    bend = (lax.broadcasted_iota(jnp.int32, (rows, nb), 1) + 1) * NSA_BLK - 1
    p = _masked_softmax_rows(s, bend <= qpos)
    oc_ref[...] = _dot(p.astype(BF16), vc)
    imp_ref[...] = jnp.sum(p.reshape(DEC_SEQ, NSA_HEADS, nb), axis=1)


def _nsa_cmp_sample(page_table, qn, pe2, w1p, w2bd, cache, layer):
    rows = DEC_SEQ * NSA_HEADS
    nb = 2 * N_PAGES

    def page_spec(p):
        return pl.BlockSpec((None, None, PAGE_SIZE, 2 * NSA_DH), lambda b, pt: (pt[b, p], layer, 0, 0))

    est = (2 * N_PAGES * PAGE_SIZE * LANE * 4 + nb * CMP_PITCH * LANE * 4
           + 2 * (NSA_BLK // 2) * 2 * LANE * LANE * 2 + (8 << 20))
    grid_spec = pltpu.PrefetchScalarGridSpec(
        num_scalar_prefetch=1,
        grid=(DEC_BATCH,),
        in_specs=[pl.BlockSpec((None, rows, NSA_DH), lambda b, pt: (b, 0, 0)),
                  pl.BlockSpec((NSA_BLK, 2 * NSA_DH), lambda b, pt: (0, 0)),
                  pl.BlockSpec((NSA_BLK // 2, 4 * NSA_DH, 2 * NSA_PHI_HID), lambda b, pt: (0, 0, 0)),
                  pl.BlockSpec((2 * NSA_PHI_HID, 2 * NSA_DH), lambda b, pt: (0, 0))]
        + [page_spec(p) for p in range(N_PAGES)],
        out_specs=(pl.BlockSpec((None, rows, NSA_DH), lambda b, pt: (b, 0, 0)),
                   pl.BlockSpec((None, DEC_SEQ, nb), lambda b, pt: (b, 0, 0))),
        scratch_shapes=[pltpu.VMEM((nb * CMP_PITCH, 2 * NSA_DH), F32)],
    )
    return pl.pallas_call(
        _nsa_cmp_sample_kernel,
        out_shape=(jax.ShapeDtypeStruct((DEC_BATCH, rows, NSA_DH), F32),
                   jax.ShapeDtypeStruct((DEC_BATCH, DEC_SEQ, nb), F32)),
        grid_spec=grid_spec,
        compiler_params=_cparams(("arbitrary",), est),
        name="nsa_cmp_sample",
    )(page_table, qn, pe2, w1p, w2bd, *([cache] * N_PAGES))


def _nsa_sel_prompt_kernel(q_ref, kv_ref, sel_ref, o_ref, m_sc, l_sc, acc_sc, *, tk):
    qi = pl.program_id(1)
    kb = pl.program_id(2)
    rows = NSA_HEADS * Q_BLOCK
    last = ((qi + 1) * Q_BLOCK - 1) // tk
    nbs = sel_ref.shape[1]

    @pl.when(kb == 0)
    def _():
        _softmax_init(m_sc, l_sc, acc_sc)

    @pl.when(kb <= last)
    def _():
        q = q_ref[...].reshape(rows, NSA_DH)
        kv = kv_ref[...]
        k = kv[:, :NSA_DH].astype(BF16)
        v = kv[:, NSA_DH:].astype(BF16)
        kpos_e = kb * tk + lax.broadcasted_iota(jnp.int32, (nbs, tk), 1)
        expand = ((kpos_e >> BLK_SHIFT) == lax.broadcasted_iota(jnp.int32, (nbs, tk), 0)).astype(BF16)
        chosen = _dot(sel_ref[...].astype(BF16), expand)
        qpos = qi * Q_BLOCK + lax.broadcasted_iota(jnp.int32, (Q_BLOCK, tk), 0)
        kpos = kb * tk + lax.broadcasted_iota(jnp.int32, (Q_BLOCK, tk), 1)
        bias = jnp.where((chosen > 0.5) & (kpos <= qpos), 0.0, NEG_INF)
        s = _dot_nt(q, k) * NSA_SCALE
        s = (s.reshape(NSA_HEADS, Q_BLOCK, tk) + bias[None]).reshape(rows, tk)
        _softmax_update(s, v, m_sc, l_sc, acc_sc)

    @pl.when(kb == pl.num_programs(2) - 1)
    def _():
        o = acc_sc[...] / l_sc[...]
        o_ref[...] = o.reshape(NSA_HEADS, Q_BLOCK, NSA_DH)


def _nsa_sel_prompt(qr, kv_sel, selmask, *, tk=512):
    nq = SEQ // Q_BLOCK
    nk = SEQ // tk
    rows = NSA_HEADS * Q_BLOCK
    nbs = selmask.shape[2]

    def kv_map(b, qi, kb):
        return (b, jnp.minimum(kb, ((qi + 1) * Q_BLOCK - 1) // tk), 0)

    est = 2 * (rows * LANE * 2 + tk * LANE * 4 + Q_BLOCK * LANE * 4 + rows * LANE * 4) + 3 * rows * LANE * 4 \
        + 5 * rows * tk * 4
    return pl.pallas_call(
        functools.partial(_nsa_sel_prompt_kernel, tk=tk),
        out_shape=jax.ShapeDtypeStruct((BATCH, NSA_HEADS, SEQ, NSA_DH), F32),
        grid=(BATCH, nq, nk),
        in_specs=[pl.BlockSpec((None, NSA_HEADS, Q_BLOCK, NSA_DH), lambda b, qi, kb: (b, 0, qi, 0)),
                  pl.BlockSpec((None, tk, 2 * NSA_DH), kv_map),
                  pl.BlockSpec((None, Q_BLOCK, nbs), lambda b, qi, kb: (b, qi, 0))],
        out_specs=pl.BlockSpec((None, NSA_HEADS, Q_BLOCK, NSA_DH), lambda b, qi, kb: (b, 0, qi, 0)),
        scratch_shapes=[pltpu.VMEM((rows, 1), F32), pltpu.VMEM((rows, 1), F32),
                        pltpu.VMEM((rows, NSA_DH), F32)],
        compiler_params=_cparams(("arbitrary", "arbitrary", "arbitrary"), est),
        name="nsa_sel_prompt",
    )(qr, kv_sel, selmask)


def _nsa_win_prompt_kernel(q_ref, kv_ref, o_ref, m_sc, l_sc, acc_sc, *, nw):
    qi = pl.program_id(1)
    w = pl.program_id(2)
    rows = NSA_HEADS * Q_BLOCK
    kt = qi - (nw - 1) + w

    @pl.when(w == 0)
    def _():
        _softmax_init(m_sc, l_sc, acc_sc)

    @pl.when(kt >= 0)
    def _():
        q = q_ref[...].reshape(rows, NSA_DH)
        kv = kv_ref[...]
        k = kv[:, :NSA_DH].astype(BF16)
        v = kv[:, NSA_DH:].astype(BF16)
        qpos = qi * Q_BLOCK + lax.broadcasted_iota(jnp.int32, (Q_BLOCK, Q_BLOCK), 0)
        kpos = kt * Q_BLOCK + lax.broadcasted_iota(jnp.int32, (Q_BLOCK, Q_BLOCK), 1)
        bias = jnp.where((kpos <= qpos) & (kpos > qpos - NSA_WINDOW), 0.0, NEG_INF)
        s = _dot_nt(q, k) * NSA_SCALE
        s = (s.reshape(NSA_HEADS, Q_BLOCK, Q_BLOCK) + bias[None]).reshape(rows, Q_BLOCK)
        _softmax_update(s, v, m_sc, l_sc, acc_sc)

    @pl.when(w == nw - 1)
    def _():
        o = acc_sc[...] / l_sc[...]
        o_ref[...] = o.reshape(NSA_HEADS, Q_BLOCK, NSA_DH)


def _nsa_win_prompt(qr, kv_win):
    nq = SEQ // Q_BLOCK
    nw = NSA_WINDOW // Q_BLOCK + 1
    rows = NSA_HEADS * Q_BLOCK
    est = 2 * (rows * LANE * 2 + Q_BLOCK * LANE * 4 + rows * LANE * 4) + 3 * rows * LANE * 4 \
        + 5 * rows * Q_BLOCK * 4
    return pl.pallas_call(
        functools.partial(_nsa_win_prompt_kernel, nw=nw),
        out_shape=jax.ShapeDtypeStruct((BATCH, NSA_HEADS, SEQ, NSA_DH), F32),
        grid=(BATCH, nq, nw),
        in_specs=[pl.BlockSpec((None, NSA_HEADS, Q_BLOCK, NSA_DH), lambda b, qi, w: (b, 0, qi, 0)),
                  pl.BlockSpec((None, Q_BLOCK, 2 * NSA_DH),
                               lambda b, qi, w: (b, jnp.maximum(qi - (nw - 1) + w, 0), 0))],
        out_specs=pl.BlockSpec((None, NSA_HEADS, Q_BLOCK, NSA_DH), lambda b, qi, w: (b, 0, qi, 0)),
        scratch_shapes=[pltpu.VMEM((rows, 1), F32), pltpu.VMEM((rows, 1), F32),
                        pltpu.VMEM((rows, NSA_DH), F32)],
        compiler_params=_cparams(("arbitrary", "arbitrary", "arbitrary"), est),
        name="nsa_win_prompt",
    )(qr, kv_win)


def _nsa_selwin_sample_kernel(pt_ref, q_ref, allowed_ref, selnew_ref, win_ref, winnew_ref, *rest):
    del pt_ref
    pages = rest[:N_PAGES]
    os_ref, ow_ref = rest[N_PAGES:]
    rows = DEC_SEQ * NSA_HEADS
    q = q_ref[...]
    kv = jnp.concatenate([p[...] for p in pages] + [selnew_ref[...]], axis=0)
    nk = kv.shape[0]
    k = kv[:, :NSA_DH].astype(BF16)
    v = kv[:, NSA_DH:].astype(BF16)
    allowed = jnp.broadcast_to(allowed_ref[...][:, None, :], (DEC_SEQ, NSA_HEADS, nk)).reshape(rows, nk)
    p = _masked_softmax_rows(_dot_nt(q, k) * NSA_SCALE, allowed > 0.5)
    os_ref[...] = _dot(p.astype(BF16), v)
    kvw = jnp.concatenate([win_ref[...], winnew_ref[...]], axis=0)
    nkw = kvw.shape[0]
    wb = win_ref.shape[0]
    kw = kvw[:, :NSA_DH].astype(BF16)
    vw = kvw[:, NSA_DH:].astype(BF16)
    t = lax.broadcasted_iota(jnp.int32, (rows, nkw), 0) >> HEAD_SHIFT
    j = lax.broadcasted_iota(jnp.int32, (rows, nkw), 1)
    kp = PAST_LEN - wb + j
    qp = PAST_LEN + t
    wmask = (kp <= qp) & (kp > qp - NSA_WINDOW) & (kp >= 0) & (j < wb + DEC_SEQ)
    pw = _masked_softmax_rows(_dot_nt(q, kw) * NSA_SCALE, wmask)
    ow_ref[...] = _dot(pw.astype(BF16), vw)


def _nsa_selwin_sample(page_table, qr, allowed, sel_new, win_state, win_new, cache, layer):
    rows = DEC_SEQ * NSA_HEADS
    nk = PAST_LEN + PAGE_SIZE
    wb = win_state.shape[2]

    def page_spec(p):
        return pl.BlockSpec((None, None, PAGE_SIZE, 2 * NSA_DH), lambda b, pt: (pt[b, p], layer, 0, 0))

    est = 2 * (N_PAGES + 2) * PAGE_SIZE * LANE * 4 + 2 * wb * LANE * 4 + 3 * nk * LANE * 4 + 8 * rows * nk * 4
    grid_spec = pltpu.PrefetchScalarGridSpec(
        num_scalar_prefetch=1,
        grid=(DEC_BATCH,),
        in_specs=[pl.BlockSpec((None, rows, NSA_DH), lambda b, pt: (b, 0, 0)),
                  pl.BlockSpec((None, DEC_SEQ, nk), lambda b, pt: (b, 0, 0)),
                  pl.BlockSpec((None, PAGE_SIZE, 2 * NSA_DH), lambda b, pt: (b, 0, 0)),
                  pl.BlockSpec((None, None, wb, 2 * NSA_DH), lambda b, pt: (b, layer, 0, 0)),
                  pl.BlockSpec((None, PAGE_SIZE, 2 * NSA_DH), lambda b, pt: (b, 0, 0))]
        + [page_spec(p) for p in range(N_PAGES)],
        out_specs=(pl.BlockSpec((None, rows, NSA_DH), lambda b, pt: (b, 0, 0)),
                   pl.BlockSpec((None, rows, NSA_DH), lambda b, pt: (b, 0, 0))),
    )
    return pl.pallas_call(
        _nsa_selwin_sample_kernel,
        out_shape=(jax.ShapeDtypeStruct((DEC_BATCH, rows, NSA_DH), F32),
                   jax.ShapeDtypeStruct((DEC_BATCH, rows, NSA_DH), F32)),
        grid_spec=grid_spec,
        compiler_params=_cparams(("arbitrary",), est),
        name="nsa_selwin_sample",
    )(page_table, qr, allowed, sel_new, win_state, win_new, *([cache] * N_PAGES))


def _moe_kernel(be_ref, nu_ref, x_ref, sw_ref, wgu_ref, bgu_ref, wdn_ref, bdn_ref, o_ref):
    del be_ref
    i = pl.program_id(0)

    @pl.when(i < nu_ref[0])
    def _():
        h = _dot(x_ref[...], wgu_ref[...]) + bgu_ref[...]
        gate = jnp.minimum(h[:, :D_FF], SWIGLU_LIMIT)
        up = jnp.clip(h[:, D_FF:], -SWIGLU_LIMIT, SWIGLU_LIMIT)
        act = gate * jax.nn.sigmoid(SWIGLU_ALPHA * gate) * (up + 1.0)
        y = _dot(act.astype(BF16), wdn_ref[...]) + bdn_ref[...]
        o_ref[...] = y * sw_ref[...]

    @pl.when(i >= nu_ref[0])
    def _():
        o_ref[...] = jnp.zeros(o_ref.shape, F32)


def _moe_experts(blk_e, n_used, xs, slot_w, w_gu, b_gu, w_dn, b_dn):
    tm = MOE_TM

    def row_map(i, be, nu):
        return (jnp.minimum(i, nu[0] - 1), 0)

    def exp_map(i, be, nu):
        return (be[jnp.minimum(i, nu[0] - 1)], 0, 0)

    est = 2 * (tm * D_MODEL * 2 + tm * LANE * 4 + D_MODEL * 2 * D_FF * 2 + D_FF * D_MODEL * 2
               + tm * D_MODEL * 4) + 6 * tm * 2 * D_FF * 4
    grid_spec = pltpu.PrefetchScalarGridSpec(
        num_scalar_prefetch=2,
        grid=(MOE_BLOCKS,),
        in_specs=[pl.BlockSpec((tm, D_MODEL), row_map),
                  pl.BlockSpec((tm, 1), row_map),
                  pl.BlockSpec((None, D_MODEL, 2 * D_FF), exp_map),
                  pl.BlockSpec((None, 1, 2 * D_FF), exp_map),
                  pl.BlockSpec((None, D_FF, D_MODEL), exp_map),
                  pl.BlockSpec((None, 1, D_MODEL), exp_map)],
        out_specs=pl.BlockSpec((tm, D_MODEL), lambda i, be, nu: (i, 0)),
    )
    return pl.pallas_call(
        _moe_kernel,
        out_shape=jax.ShapeDtypeStruct((MOE_SLOTS, D_MODEL), F32),
        grid_spec=grid_spec,
        compiler_params=_cparams(("arbitrary",), est),
        name="moe_experts",
    )(blk_e, n_used, xs, slot_w, w_gu, b_gu, w_dn, b_dn)


def _rms_norm(x, g):
    return x * lax.rsqrt(jnp.mean(x * x, axis=-1, keepdims=True) + RMS_EPS) * g


def _layer_norm(x, g, b):
    xc = x - jnp.mean(x, axis=-1, keepdims=True)
    var = jnp.mean(xc * xc, axis=-1, keepdims=True)
    return xc * lax.rsqrt(var + LN_EPS) * g + b


def _rope(x, pos, rot_dim):
    half = rot_dim // 2
    freqs = ROPE_THETA ** (-jnp.arange(half, dtype=F32) * (2.0 / rot_dim))
    ang = pos.astype(F32)[:, None] * freqs[None, :]
    shape = (ang.shape[0],) + (1,) * (x.ndim - 2) + (half,)
    cos = jnp.cos(ang).reshape(shape)
    sin = jnp.sin(ang).reshape(shape)
    x1, x2, rest = x[..., :half], x[..., half:rot_dim], x[..., rot_dim:]
    return jnp.concatenate([x1 * cos - x2 * sin, x2 * cos + x1 * sin, rest], axis=-1)


def _pad_rows(x, n):
    return jnp.pad(x, ((0, 0), (0, n - x.shape[1]), (0, 0)))


def _select_blocks(imp, q_pos, nbs):
    nb = imp.shape[-1]
    imp = jnp.pad(imp, ((0, 0), (0, 0), (0, nbs - nb)), constant_values=-1.0)
    blk = jnp.arange(nbs)[None, :]
    cur = (q_pos // NSA_BLK)[:, None]
    forced = (blk == 0) | (blk == cur) | (blk == cur - 1)
    score = jnp.where(blk > cur, -1.0, jnp.where(forced, FORCE_SCORE, imp))
    n_sel = min(NSA_TOPN, nbs)
    top_s, top_i = lax.top_k(score, n_sel)
    onehot = (top_i[..., None] == jnp.arange(nbs)) & (top_s >= 0.0)[..., None]
    return jnp.any(onehot, axis=-2).astype(F32)


def _moe(x, router_w, router_b, w_gu, b_gu, w_dn, b_dn, layer):
    n_tok = x.shape[0]
    n_asg = n_tok * TOP_K
    logits = _mm_f32(x, router_w, tm=512, name=f"router_l{layer}") + router_b
    top_v, top_e = lax.top_k(logits, TOP_K)
    gates = jax.nn.softmax(top_v, axis=-1)
    flat_e = top_e.reshape(-1)
    order = jnp.argsort(flat_e)
    e_sorted = flat_e[order]
    tok_sorted = (order // TOP_K).astype(jnp.int32)
    w_sorted = gates.reshape(-1)[order]
    counts = jnp.bincount(flat_e, length=N_EXPERTS)
    padded = (counts + MOE_TM - 1) // MOE_TM * MOE_TM
    pad_end = jnp.cumsum(padded)
    pad_start = pad_end - padded
    grp_start = jnp.cumsum(counts) - counts
    dest = (pad_start[e_sorted] + jnp.arange(n_asg) - grp_start[e_sorted]).astype(jnp.int32)
    slot_tok = jnp.zeros((MOE_SLOTS,), jnp.int32).at[dest].set(tok_sorted)
    slot_w = jnp.zeros((MOE_SLOTS,), F32).at[dest].set(w_sorted)
    blk_e = jnp.minimum(jnp.searchsorted(pad_end, jnp.arange(MOE_BLOCKS) * MOE_TM, side='right'),
                        N_EXPERTS - 1).astype(jnp.int32)
    n_used = (pad_end[-1] // MOE_TM).astype(jnp.int32).reshape(1)
    xs = x.astype(BF16)[slot_tok]
    ys = _moe_experts(blk_e, n_used, xs, slot_w[:, None], w_gu, b_gu[:, None, :], w_dn, b_dn[:, None, :])
    slot_of = jnp.zeros((n_asg,), jnp.int32).at[order].set(dest).reshape(n_tok, TOP_K)
    return jnp.sum(ys[slot_of], axis=1)


def _prep_weights(w_in, mla_w_uq, mla_w_uk, mla_w_uv, nsa_phi_w1, nsa_phi_w2):
    d = w_in.shape[0]
    z = lambda n: jnp.zeros((d, D_MODEL, n), F32)
    w_in_p = jnp.concatenate([w_in[:, :, :672], z(96), w_in[:, :, 672:4176], z(80), w_in[:, :, 4176:],
                              z(IN_PAD_N - IN_GM - 3 * D_MODEL)], axis=2).astype(BF16)
    uq = mla_w_uq.reshape(d, MLA_Q_LORA, MLA_HEADS, MLA_NOPE + MLA_ROPE)
    w_uq_p = jnp.concatenate([uq[..., :MLA_NOPE].reshape(d, MLA_Q_LORA, -1),
                              uq[..., MLA_NOPE:].reshape(d, MLA_Q_LORA, -1)], axis=2).astype(BF16)
    w_uk_t = jnp.transpose(mla_w_uk, (0, 2, 3, 1)).astype(BF16)
    w_uv_t = jnp.transpose(mla_w_uv, (0, 2, 1, 3)).astype(BF16)
    w1 = nsa_phi_w1.reshape(d, 2, NSA_BLK, NSA_DH, NSA_PHI_HID)
    zero = jnp.zeros_like(w1[:, 0])
    w1k = jnp.concatenate([w1[:, 0], zero], axis=-1)
    w1v = jnp.concatenate([zero, w1[:, 1]], axis=-1)
    w1r = jnp.concatenate([w1k, w1v], axis=2)
    w1_flat = w1r.reshape(d, NSA_BLK * 2 * NSA_DH, 2 * NSA_PHI_HID).astype(BF16)
    w1_pair = w1r.reshape(d, NSA_BLK // 2, 4 * NSA_DH, 2 * NSA_PHI_HID).astype(BF16)
    z2 = jnp.zeros((d, NSA_PHI_HID, NSA_DH), F32)
    w2bd = jnp.concatenate([jnp.concatenate([nsa_phi_w2[:, 0], z2], axis=2),
                            jnp.concatenate([z2, nsa_phi_w2[:, 1]], axis=2)], axis=1).astype(BF16)
    return w_in_p, w_uq_p, w_uk_t, w_uv_t, w1_flat, w1_pair, w2bd


def kernel(x_prompt, x_sample, cache_mla, cache_nsa_cmp, cache_nsa_sel, state_nsa_win, state_conv, page_table,
           w_in, mla_q_norm, mla_kv_norm, mla_w_uq, mla_w_uk, mla_w_uv, mla_w_br,
           conv_w, conv_b, conv_ln_g, conv_ln_b, conv_w_br,
           nsa_phi_pe, nsa_phi_w1, nsa_phi_w2, nsa_w_br, w_out,
           ln1_g, ln1_b, ln2_g, ln2_b, router_w, router_b, moe_w_gu, moe_b_gu, moe_w_dn, moe_b_dn):
    w_in_p, w_uq_p, w_uk_t, w_uv_t, w1_flat, w1_pair, w2bd = _prep_weights(
        w_in, mla_w_uq, mla_w_uk, mla_w_uv, nsa_phi_w1, nsa_phi_w2)
    pos = jnp.concatenate([jnp.tile(jnp.arange(SEQ), BATCH),
                           jnp.tile(PAST_LEN + jnp.arange(DEC_SEQ), DEC_BATCH)])
    q_pos_p = jnp.arange(SEQ)
    q_pos_s = PAST_LEN + jnp.arange(DEC_SEQ)
    nbs_s = -(-(PAST_LEN + DEC_SEQ) // NSA_BLK)
    mp = M_PROMPT
    x = jnp.concatenate([x_prompt.reshape(mp, D_MODEL), x_sample.reshape(M_SAMPLE, D_MODEL)], axis=0)
    st_p, st_s = [], []
    for l in range(DEPTH):
        h = _mm(x, w_in_p[l], tm=512, tn=1536, name=f"in_proj_l{l}")
        cq = _rms_norm(h[:, IN_CQ:IN_CQ + MLA_Q_LORA], mla_q_norm[l])
        q = _mm(cq, w_uq_p[l], tm=1088, tn=1536, name=f"mla_uq_l{l}")
        q_lat = _head_up(q[:, :MLA_HEADS * MLA_NOPE], w_uk_t[l], tm=1088, out_dtype=BF16, name=f"mla_uk_l{l}")
        q_rope = _rope(q[:, MLA_HEADS * MLA_NOPE:].reshape(M_ALL, MLA_HEADS, MLA_ROPE), pos, MLA_ROPE)
        q_rope = jnp.transpose(q_rope, (1, 0, 2)).astype(BF16)
        mla_rows = jnp.concatenate([_rms_norm(h[:, IN_CKV:IN_CKV + MLA_KV_LORA], mla_kv_norm[l]),
                                    _rope(h[:, IN_KR:IN_KR + MLA_ROPE], pos, MLA_ROPE)], axis=1)
        mla_rows_p = mla_rows[:mp].reshape(BATCH, SEQ, MLA_CACHE)
        mla_rows_s = mla_rows[mp:].reshape(DEC_BATCH, DEC_SEQ, MLA_CACHE)
        o_lat_p = _mla_prompt(q_lat[:, :mp].reshape(MLA_HEADS, BATCH, SEQ, MLA_KV_LORA),
                              q_rope[:, :mp].reshape(MLA_HEADS, BATCH, SEQ, MLA_ROPE), mla_rows_p)

        def to_rows(a):
            dd = a.shape[-1]
            return jnp.transpose(a.reshape(MLA_HEADS, DEC_BATCH, DEC_SEQ, dd), (1, 2, 0, 3)).reshape(
                DEC_BATCH, DEC_SEQ * MLA_HEADS, dd)

        o_lat_s = _mla_sample(page_table, to_rows(q_lat[:, mp:]), to_rows(q_rope[:, mp:]),
                              _pad_rows(mla_rows_s, PAGE_SIZE), cache_mla, l)
        o_lat_s = jnp.transpose(o_lat_s.reshape(DEC_BATCH, DEC_SEQ, MLA_HEADS, MLA_KV_LORA),
                                (2, 0, 1, 3)).reshape(MLA_HEADS, M_SAMPLE, MLA_KV_LORA)
        o_lat = jnp.concatenate([o_lat_p.reshape(MLA_HEADS, mp, MLA_KV_LORA), o_lat_s], axis=1)
        o_mla = _head_down(o_lat, w_uv_t[l], tm=1088, out_dtype=BF16, name=f"mla_uv_l{l}")
        y_a = _mm(o_mla, mla_w_br[l].astype(BF16), tm=1088, tn=1024, name=f"mla_br_l{l}")
        glu = h[:, IN_GLU:IN_GLU + 2 * CONV_DIM]
        u = glu[:, :CONV_DIM] * jax.nn.sigmoid(glu[:, CONV_DIM:])
        u_p = u[:mp].reshape(BATCH, SEQ, CONV_DIM)
        u_s = u[mp:].reshape(DEC_BATCH, DEC_SEQ, CONV_DIM)
        c_p = _conv_prompt(u_p, conv_w[l], conv_b[l], conv_ln_g[l], conv_ln_b[l])
        u_ext = jnp.concatenate([state_conv[:, l], u_s], axis=1)
        y_s = sum(u_ext[:, k:k + DEC_SEQ] * conv_w[l][k] for k in range(CONV_WIDTH)) + conv_b[l]
        c_s = jax.nn.silu(_layer_norm(y_s, conv_ln_g[l], conv_ln_b[l])).astype(BF16)
        c_all = jnp.concatenate([c_p.reshape(mp, CONV_DIM), c_s.reshape(M_SAMPLE, CONV_DIM)], axis=0)
        y_conv = _mm(c_all, conv_w_br[l].astype(BF16), tm=1088, tn=1024, name=f"conv_br_l{l}")
        qn = h[:, IN_QN:IN_QN + NSA_HEADS * NSA_DH]
        qr = _rope(qn.reshape(M_ALL, NSA_HEADS, NSA_DH), pos, NSA_ROT)
        kv_cmp = h[:, IN_CMP:IN_CMP + 2 * NSA_DH]
        kv_sel = h[:, IN_SEL:IN_SEL + 2 * NSA_DH]
        kv_sel = jnp.concatenate([_rope(kv_sel[:, :NSA_DH], pos, NSA_ROT), kv_sel[:, NSA_DH:]], axis=1)
        kv_win = h[:, IN_WIN:IN_WIN + 2 * NSA_DH]
        kv_win = jnp.concatenate([_rope(kv_win[:, :NSA_DH], pos, NSA_ROT), kv_win[:, NSA_DH:]], axis=1)
        g_nsa = jax.nn.sigmoid(h[:, IN_GN:IN_GN + 3 * NSA_HEADS]).reshape(M_ALL, NSA_HEADS, 3)
        pe2 = nsa_phi_pe[l].reshape(NSA_BLK, 2 * NSA_DH)
        kv_cmp_p = kv_cmp[:mp].reshape(BATCH, SEQ, 2 * NSA_DH)
        nb_p = SEQ // NSA_BLK
        flat = (kv_cmp_p.reshape(BATCH, nb_p, NSA_BLK, 2 * NSA_DH) + pe2).reshape(BATCH * nb_p, -1)
        hid = jax.nn.gelu(_mm(flat, w1_flat[l], tm=BATCH * nb_p, tn=2 * NSA_PHI_HID, name=f"nsa_phi1_l{l}"))
        kcvc_p = _mm(hid, w2bd[l], tm=BATCH * nb_p, tn=2 * NSA_DH, name=f"nsa_phi2_l{l}")
        o_c_p, imp_p = _nsa_cmp_prompt(qn[:mp].reshape(BATCH, SEQ, -1), kcvc_p.reshape(BATCH, nb_p, 2 * NSA_DH))
        selmask_p = _select_blocks(imp_p, q_pos_p, nb_p)
        qr_p = jnp.transpose(qr[:mp].reshape(BATCH, SEQ, NSA_HEADS, NSA_DH), (0, 2, 1, 3)).astype(BF16)
        kv_sel_p = kv_sel[:mp].reshape(BATCH, SEQ, 2 * NSA_DH)
        kv_win_p = kv_win[:mp].reshape(BATCH, SEQ, 2 * NSA_DH)
        o_s_p = _nsa_sel_prompt(qr_p, kv_sel_p, selmask_p)
        o_w_p = _nsa_win_prompt(qr_p, kv_win_p)
        to_tok = lambda a: jnp.transpose(a, (0, 2, 1, 3)).reshape(mp, NSA_HEADS, NSA_DH)
        o_c_p = o_c_p.reshape(mp, NSA_HEADS, NSA_DH)
        o_s_p, o_w_p = to_tok(o_s_p), to_tok(o_w_p)
        kv_sel_s = kv_sel[mp:].reshape(DEC_BATCH, DEC_SEQ, 2 * NSA_DH)
        kv_win_s = kv_win[mp:].reshape(DEC_BATCH, DEC_SEQ, 2 * NSA_DH)
        qn_s = qn[mp:].reshape(DEC_BATCH, DEC_SEQ * NSA_HEADS, NSA_DH).astype(BF16)
        qr_s = qr[mp:].reshape(DEC_BATCH, DEC_SEQ * NSA_HEADS, NSA_DH).astype(BF16)
        o_c_s, imp_s = _nsa_cmp_sample(page_table, qn_s, pe2, w1_pair[l], w2bd[l], cache_nsa_cmp, l)
        selmask_s = _select_blocks(imp_s, q_pos_s, nbs_s)
        key_pos = jnp.arange(PAST_LEN + PAGE_SIZE)
        allowed = jnp.repeat(selmask_s[:, :, :nbs_s - 1], NSA_BLK, axis=2)
        new_ok = selmask_s[:, :, nbs_s - 1:] * (key_pos[None, None, PAST_LEN:] <= q_pos_s[None, :, None])
        allowed = jnp.concatenate([allowed, new_ok.astype(F32)], axis=2)
        o_s_s, o_w_s = _nsa_selwin_sample(page_table, qr_s, allowed, _pad_rows(kv_sel_s, PAGE_SIZE),
                                          state_nsa_win, _pad_rows(kv_win_s, PAGE_SIZE), cache_nsa_sel, l)
        cat = lambda a, b: jnp.concatenate([a, b.reshape(M_SAMPLE, NSA_HEADS, NSA_DH)], axis=0)
        o_nsa = (g_nsa[..., 0:1] * cat(o_c_p, o_c_s) + g_nsa[..., 1:2] * cat(o_s_p, o_s_s)
                 + g_nsa[..., 2:3] * cat(o_w_p, o_w_s)).reshape(M_ALL, NSA_HEADS * NSA_DH)
        y_n = _mm(o_nsa, nsa_w_br[l].astype(BF16), tm=1088, tn=1024, name=f"nsa_br_l{l}")
        gm = jax.nn.sigmoid(h[:, IN_GM:IN_GM + 3 * D_MODEL])
        m = gm[:, :D_MODEL] * y_a + gm[:, D_MODEL:2 * D_MODEL] * y_conv + gm[:, 2 * D_MODEL:] * y_n
        mix = _mm(m, w_out[l].astype(BF16), tm=1088, tn=1024, name=f"w_out_l{l}")
        x1 = _layer_norm(DN_ALPHA * x + mix, ln1_g[l], ln1_b[l])
        f = _moe(x1, router_w[l], router_b[l], moe_w_gu[l].astype(BF16), moe_b_gu[l],
                 moe_w_dn[l].astype(BF16), moe_b_dn[l], l)
        x = _layer_norm(DN_ALPHA * x1 + f, ln2_g[l], ln2_b[l])
        win_all = jnp.concatenate([state_nsa_win[:, l], kv_win_s], axis=1)
        st_p.append((mla_rows_p, kv_cmp_p, kv_sel_p, kv_win_p[:, -min(NSA_WINDOW, SEQ):],
                     u_p[:, -(CONV_WIDTH - 1):]))
        st_s.append((mla_rows_s, kv_cmp[mp:].reshape(DEC_BATCH, DEC_SEQ, 2 * NSA_DH), kv_sel_s,
                     win_all[:, -state_nsa_win.shape[2]:], u_ext[:, -(CONV_WIDTH - 1):]))
    stack = lambda sts, i: jnp.stack([s[i] for s in sts], axis=1)
    return (x[:mp].reshape(BATCH, SEQ, D_MODEL), x[mp:].reshape(DEC_BATCH, DEC_SEQ, D_MODEL),
            stack(st_p, 0), stack(st_p, 1), stack(st_p, 2), stack(st_p, 3), stack(st_p, 4),
            stack(st_s, 0), stack(st_s, 1), stack(st_s, 2), stack(st_s, 3), stack(st_s, 4))
```

```python
import functools

import numpy as np
import jax
import jax.numpy as jnp
from jax import lax
from jax.experimental import pallas as pl
from jax.experimental.pallas import tpu as pltpu

F32 = jnp.float32
BF16 = jnp.bfloat16

D_MODEL = 2048
BATCH = 2
SEQ = 4096
DEPTH = 4
DEC_BATCH = 128
DEC_SEQ = 4
PAST_LEN = 8192
PAGE_SIZE = 128
N_PAGES = PAST_LEN // PAGE_SIZE

MLA_HEADS = 16
MLA_NOPE = 64
MLA_ROPE = 32
MLA_V = 64
MLA_Q_LORA = 384
MLA_KV_LORA = 256
MLA_CACHE = MLA_KV_LORA + MLA_ROPE
MLA_SCALE = (MLA_NOPE + MLA_ROPE) ** -0.5
CONV_DIM = 1024
CONV_WIDTH = 31
NSA_HEADS = 16
NSA_DH = 64
NSA_ROT = NSA_DH // 4
NSA_BLK = 64
NSA_TOPN = 16
NSA_WINDOW = 512
NSA_PHI_HID = 64
NSA_SCALE = NSA_DH ** -0.5
N_EXPERTS = 32
TOP_K = 4
D_FF = 768
SWIGLU_LIMIT = 7.0
SWIGLU_ALPHA = 1.702
ROPE_THETA = 500000.0
Q_BLOCK = 128
DN_ALPHA = (2 * DEPTH) ** 0.25
NEG_INF = -1e30
FORCE_SCORE = 1e6
RMS_EPS = 1e-6
LN_EPS = 1e-5

M_PROMPT = BATCH * SEQ
M_SAMPLE = DEC_BATCH * DEC_SEQ
M_ALL = M_PROMPT + M_SAMPLE

V7X_VMEM_LIMIT_CAP = 60000 * 1024
LANE = 128

IN_CQ = 0
IN_CKV = 384
IN_KR = 640
IN_GLU = 768
IN_QN = 2816
IN_CMP = 3840
IN_SEL = 3968
IN_WIN = 4096
IN_GN = 4224
IN_GM = 4352
IN_PAD_N = 10752

MOE_TM = 256
MOE_BLOCKS = (M_ALL * TOP_K) // MOE_TM + N_EXPERTS
MOE_SLOTS = MOE_BLOCKS * MOE_TM

CMP_PITCH = 72
HEAD_SHIFT = 4
BLK_SHIFT = 6
assert (1 << HEAD_SHIFT) == MLA_HEADS == NSA_HEADS and (1 << BLK_SHIFT) == NSA_BLK
assert NSA_SCALE == 2.0 ** -3


def _cparams(sem, est_bytes):
    limit = int(min(max(est_bytes * 5 // 4 + (4 << 20), 32 << 20), V7X_VMEM_LIMIT_CAP))
    return pltpu.CompilerParams(dimension_semantics=sem, vmem_limit_bytes=limit)


def _dot(a, b):
    return jnp.dot(a, b, preferred_element_type=F32)


def _dot_nt(a, b):
    return lax.dot_general(a, b, (((1,), (1,)), ((), ())), preferred_element_type=F32)


def _mm_kernel(x_ref, w_ref, o_ref):
    o_ref[...] = _dot(x_ref[...].astype(BF16), w_ref[...].astype(BF16)).astype(o_ref.dtype)


def _mm(x, w, *, tm, tn, out_dtype=F32, name):
    m, k = x.shape
    n = w.shape[1]
    est = 2 * (tm * k * x.dtype.itemsize + k * tn * w.dtype.itemsize + tm * tn * 4) + tm * k * 2 + k * tn * 2
    return pl.pallas_call(
        _mm_kernel,
        out_shape=jax.ShapeDtypeStruct((m, n), out_dtype),
        grid=(pl.cdiv(n, tn), pl.cdiv(m, tm)),
        in_specs=[pl.BlockSpec((tm, k), lambda j, i: (i, 0)),
                  pl.BlockSpec((k, tn), lambda j, i: (0, j))],
        out_specs=pl.BlockSpec((tm, tn), lambda j, i: (i, j)),
        compiler_params=_cparams(("arbitrary", "arbitrary"), est),
        name=name,
    )(x, w)


def _mm_f32_kernel(x_ref, w_ref, o_ref):
    o_ref[...] = jnp.dot(x_ref[...], w_ref[...], preferred_element_type=F32,
                         precision=lax.Precision.HIGHEST)


def _mm_f32(x, w, *, tm, name):
    m, k = x.shape
    n = w.shape[1]
    est = 2 * (tm * k * 4 + k * LANE * 4 + tm * LANE * 4)
    return pl.pallas_call(
        _mm_f32_kernel,
        out_shape=jax.ShapeDtypeStruct((m, n), F32),
        grid=(pl.cdiv(m, tm),),
        in_specs=[pl.BlockSpec((tm, k), lambda i: (i, 0)),
                  pl.BlockSpec((k, n), lambda i: (0, 0))],
        out_specs=pl.BlockSpec((tm, n), lambda i: (i, 0)),
        compiler_params=_cparams(("arbitrary",), est),
        name=name,
    )(x, w)


def _head_up_kernel(x_ref, w_ref, o_ref):
    kh = w_ref.shape[1]
    x = x_ref[...].astype(BF16)
    for j in range(2):
        o_ref[j] = _dot(x[:, j * kh:(j + 1) * kh], w_ref[j]).astype(o_ref.dtype)


def _head_up(x, w, *, tm, out_dtype, name):
    m = x.shape[0]
    nheads, kh, nh = w.shape
    est = 2 * (tm * 2 * kh * 4 + 2 * kh * nh * 2 + 2 * tm * nh * 4)
    return pl.pallas_call(
        _head_up_kernel,
        out_shape=jax.ShapeDtypeStruct((nheads, m, nh), out_dtype),
        grid=(nheads // 2, pl.cdiv(m, tm)),
        in_specs=[pl.BlockSpec((tm, 2 * kh), lambda h, i: (i, h)),
                  pl.BlockSpec((2, kh, nh), lambda h, i: (h, 0, 0))],
        out_specs=pl.BlockSpec((2, tm, nh), lambda h, i: (h, i, 0)),
        compiler_params=_cparams(("arbitrary", "arbitrary"), est),
        name=name,
    )(x, w)


def _head_down_kernel(x_ref, w_ref, o_ref):
    o_ref[...] = jnp.concatenate(
        [_dot(x_ref[j].astype(BF16), w_ref[j]) for j in range(2)], axis=1).astype(o_ref.dtype)


def _head_down(x, w, *, tm, out_dtype, name):
    nheads, m, kh = x.shape
    nh = w.shape[2]
    est = 2 * (2 * tm * kh * 4 + 2 * kh * nh * 2 + tm * 2 * nh * 4)
    return pl.pallas_call(
        _head_down_kernel,
        out_shape=jax.ShapeDtypeStruct((m, nheads * nh), out_dtype),
        grid=(nheads // 2, pl.cdiv(m, tm)),
        in_specs=[pl.BlockSpec((2, tm, kh), lambda h, i: (h, i, 0)),
                  pl.BlockSpec((2, kh, nh), lambda h, i: (h, 0, 0))],
        out_specs=pl.BlockSpec((tm, 2 * nh), lambda h, i: (i, h)),
        compiler_params=_cparams(("arbitrary", "arbitrary"), est),
        name=name,
    )(x, w)


def _softmax_init(m_sc, l_sc, acc_sc):
    m_sc[...] = jnp.full(m_sc.shape, NEG_INF, F32)
    l_sc[...] = jnp.zeros(l_sc.shape, F32)
    acc_sc[...] = jnp.zeros(acc_sc.shape, F32)


def _softmax_update(s, v, m_sc, l_sc, acc_sc, *, v_transposed=False):
    m_prev = m_sc[...]
    m_new = jnp.maximum(m_prev, jnp.max(s, axis=1, keepdims=True))
    alpha = jnp.exp(m_prev - m_new)
    p = jnp.exp(s - m_new)
    pb = p.astype(BF16)
    pv = _dot_nt(pb, v) if v_transposed else _dot(pb, v)
    l_sc[...] = alpha * l_sc[...] + jnp.sum(p, axis=1, keepdims=True)
    acc_sc[...] = alpha * acc_sc[...] + pv
    m_sc[...] = m_new


FLASH_SUB = 64


def _flash_rows(s_sc, p_sc, bias_sc, m_sc, l_sc, acc_sc, *, scale, n_groups, group_rows):
    def group(g, carry):
        for j in range(group_rows // FLASH_SUB):
            rs = pl.ds(pl.multiple_of(g * group_rows + j * FLASH_SUB, FLASH_SUB), FLASH_SUB)
            s = s_sc[rs, :]
            if scale != 1.0:
                s = s * scale
            s = s + bias_sc[j * FLASH_SUB:(j + 1) * FLASH_SUB, :]
            m_prev = m_sc[rs, :]
            m_new = jnp.maximum(m_prev, jnp.max(s, axis=1, keepdims=True))
            alpha = jnp.exp(m_prev - m_new)
            p = jnp.exp(s - m_new)
            l_sc[rs, :] = alpha * l_sc[rs, :] + jnp.sum(p, axis=1, keepdims=True)
            acc_sc[rs, :] = alpha * acc_sc[rs, :]
            m_sc[rs, :] = m_new
            p_sc[rs, :] = p.astype(BF16)
        return carry

    lax.fori_loop(0, n_groups, group, 0)


def _mla_prompt_kernel(ql_ref, qr_ref, kv_ref, o_ref, m_sc, l_sc, acc_sc, s_sc, p_sc, bias_sc, *, tk):
    qi = pl.program_id(1)
    kb = pl.program_id(2)
    rows = MLA_HEADS * Q_BLOCK
    last = ((qi + 1) * Q_BLOCK - 1) // tk

    @pl.when(kb == 0)
    def _():
        _softmax_init(m_sc, l_sc, acc_sc)

    @pl.when(kb <= last)
    def _():
        ql = ql_ref[...].reshape(rows, MLA_KV_LORA)
        qr = qr_ref[...].reshape(rows, MLA_ROPE)
        kv = kv_ref[...]
        c = kv[:, :MLA_KV_LORA].astype(BF16)
        kr = kv[:, MLA_KV_LORA:].astype(BF16)
        s_sc[...] = _dot_nt(ql, c) + _dot_nt(qr, kr)
        qpos = qi * Q_BLOCK + lax.broadcasted_iota(jnp.int32, (Q_BLOCK, tk), 0)
        kpos = kb * tk + lax.broadcasted_iota(jnp.int32, (Q_BLOCK, tk), 1)
        bias_sc[...] = jnp.where(kpos <= qpos, 0.0, NEG_INF)
        _flash_rows(s_sc, p_sc, bias_sc, m_sc, l_sc, acc_sc, scale=MLA_SCALE,
                    n_groups=MLA_HEADS, group_rows=Q_BLOCK)
        acc_sc[...] += _dot(p_sc[...], c)

    @pl.when(kb == pl.num_programs(2) - 1)
    def _():
        o = acc_sc[...] / l_sc[...]
        o_ref[...] = o.reshape(MLA_HEADS, Q_BLOCK, MLA_KV_LORA).astype(o_ref.dtype)


def _mla_prompt(ql, qr, kv, *, tk=512):
    nq = SEQ // Q_BLOCK
    nk = SEQ // tk
    rows = MLA_HEADS * Q_BLOCK

    def kv_map(b, qi, kb):
        return (b, jnp.minimum(kb, ((qi + 1) * Q_BLOCK - 1) // tk), 0)

    est = (2 * (rows * MLA_KV_LORA * 2 + rows * LANE * 2 + tk * 3 * LANE * 4 + rows * MLA_KV_LORA * 2)
           + rows * (MLA_KV_LORA + 2 * LANE) * 4 + 3 * rows * tk * 4 + rows * MLA_KV_LORA * 4)
    return pl.pallas_call(
        functools.partial(_mla_prompt_kernel, tk=tk),
        out_shape=jax.ShapeDtypeStruct((MLA_HEADS, BATCH, SEQ, MLA_KV_LORA), BF16),
        grid=(BATCH, nq, nk),
        in_specs=[pl.BlockSpec((MLA_HEADS, None, Q_BLOCK, MLA_KV_LORA), lambda b, qi, kb: (0, b, qi, 0)),
                  pl.BlockSpec((MLA_HEADS, None, Q_BLOCK, MLA_ROPE), lambda b, qi, kb: (0, b, qi, 0)),
                  pl.BlockSpec((None, tk, MLA_CACHE), kv_map)],
        out_specs=pl.BlockSpec((MLA_HEADS, None, Q_BLOCK, MLA_KV_LORA), lambda b, qi, kb: (0, b, qi, 0)),
        scratch_shapes=[pltpu.VMEM((rows, 1), F32), pltpu.VMEM((rows, 1), F32),
                        pltpu.VMEM((rows, MLA_KV_LORA), F32), pltpu.VMEM((rows, tk), F32),
                        pltpu.VMEM((rows, tk), BF16), pltpu.VMEM((Q_BLOCK, tk), F32)],
        compiler_params=_cparams(("arbitrary", "arbitrary", "arbitrary"), est),
        name="mla_prompt",
    )(ql, qr, kv)


def _mla_sample_kernel(pt_ref, ql_ref, qr_ref, new_ref, *rest, pp):
    del pt_ref
    pages = rest[:pp]
    o_ref, m_sc, l_sc, acc_sc = rest[pp:]
    ch = pl.program_id(1)
    rows = DEC_SEQ * MLA_HEADS

    @pl.when(ch == 0)
    def _():
        _softmax_init(m_sc, l_sc, acc_sc)

    ql = ql_ref[...]
    qr = qr_ref[...]
    kvt = jnp.concatenate([p[...] for p in pages], axis=1)
    ct = kvt[:MLA_KV_LORA].astype(BF16)
    krt = kvt[MLA_KV_LORA:].astype(BF16)
    s = (_dot(ql, ct) + _dot(qr, krt)) * MLA_SCALE
    _softmax_update(s, ct, m_sc, l_sc, acc_sc, v_transposed=True)

    @pl.when(ch == pl.num_programs(1) - 1)
    def _():
        kvn = new_ref[...]
        cn = kvn[:MLA_KV_LORA].astype(BF16)
        krn = kvn[MLA_KV_LORA:].astype(BF16)
        sn = (_dot(ql, cn) + _dot(qr, krn)) * MLA_SCALE
        t = lax.broadcasted_iota(jnp.int32, (rows, PAGE_SIZE), 0) >> HEAD_SHIFT
        j = lax.broadcasted_iota(jnp.int32, (rows, PAGE_SIZE), 1)
        sn = jnp.where(j <= t, sn, NEG_INF)
        _softmax_update(sn, cn, m_sc, l_sc, acc_sc, v_transposed=True)
        o_ref[...] = (acc_sc[...] / l_sc[...]).astype(o_ref.dtype)


def _mla_sample(page_table, ql, qr, new_rows, cache, layer, *, pp=32):
    rows = DEC_SEQ * MLA_HEADS
    nch = N_PAGES // pp

    def page_spec(p):
        return pl.BlockSpec((None, None, MLA_CACHE, PAGE_SIZE),
                            lambda b, ch, pt: (pt[b, ch * pp + p], layer, 0, 0))

    est = (2 * (pp + 1) * PAGE_SIZE * 3 * LANE * 4 + 3 * pp * PAGE_SIZE * 3 * LANE * 4
           + 6 * rows * pp * PAGE_SIZE * 4)
    grid_spec = pltpu.PrefetchScalarGridSpec(
        num_scalar_prefetch=1,
        grid=(DEC_BATCH, nch),
        in_specs=[pl.BlockSpec((None, rows, MLA_KV_LORA), lambda b, ch, pt: (b, 0, 0)),
                  pl.BlockSpec((None, rows, MLA_ROPE), lambda b, ch, pt: (b, 0, 0)),
                  pl.BlockSpec((None, MLA_CACHE, PAGE_SIZE), lambda b, ch, pt: (b, 0, 0))]
        + [page_spec(p) for p in range(pp)],
        out_specs=pl.BlockSpec((None, rows, MLA_KV_LORA), lambda b, ch, pt: (b, 0, 0)),
        scratch_shapes=[pltpu.VMEM((rows, 1), F32), pltpu.VMEM((rows, 1), F32),
                        pltpu.VMEM((rows, MLA_KV_LORA), F32)],
    )
    return pl.pallas_call(
        functools.partial(_mla_sample_kernel, pp=pp),
        out_shape=jax.ShapeDtypeStruct((DEC_BATCH, rows, MLA_KV_LORA), BF16),
        grid_spec=grid_spec,
        compiler_params=_cparams(("arbitrary", "arbitrary"), est),
        name="mla_sample",
    )(page_table, ql, qr, new_rows, *([cache] * pp))


def _conv_prompt_kernel(prev_ref, cur_ref, w_ref, b_ref, g_ref, beta_ref, o_ref, ext_sc, *, tm):
    i = pl.program_id(1)
    halo = 32
    sub = 32
    prev = prev_ref[...]
    ext_sc[0:halo, :] = jnp.where(i > 0, prev, 0.0)
    ext_sc[halo:halo + tm, :] = cur_ref[...]
    for r0 in range(0, tm, sub):
        acc = jnp.zeros((sub, CONV_DIM), F32)
        for k in range(CONV_WIDTH):
            acc = acc + ext_sc[pl.ds(r0 + halo - (CONV_WIDTH - 1) + k, sub), :] * w_ref[k:k + 1, :]
        y = acc + b_ref[...]
        mu = jnp.mean(y, axis=1, keepdims=True)
        yc = y - mu
        var = jnp.mean(yc * yc, axis=1, keepdims=True)
        z = yc * lax.rsqrt(var + LN_EPS) * g_ref[...] + beta_ref[...]
        o_ref[r0:r0 + sub, :] = (z * jax.nn.sigmoid(z)).astype(o_ref.dtype)


def _conv_prompt(u, w, b, g, beta, *, tm=256):
    halo = 32
    est = 2 * (halo + 2 * tm) * CONV_DIM * 4 + (halo + 6 * tm) * CONV_DIM * 4
    vec = pl.BlockSpec((1, CONV_DIM), lambda bb, i: (0, 0))
    return pl.pallas_call(
        functools.partial(_conv_prompt_kernel, tm=tm),
        out_shape=jax.ShapeDtypeStruct((BATCH, SEQ, CONV_DIM), BF16),
        grid=(BATCH, SEQ // tm),
        in_specs=[pl.BlockSpec((None, halo, CONV_DIM),
                               lambda bb, i: (bb, jnp.maximum(i * (tm // halo) - 1, 0), 0)),
                  pl.BlockSpec((None, tm, CONV_DIM), lambda bb, i: (bb, i, 0)),
                  pl.BlockSpec((CONV_WIDTH, CONV_DIM), lambda bb, i: (0, 0)),
                  vec, vec, vec],
        out_specs=pl.BlockSpec((None, tm, CONV_DIM), lambda bb, i: (bb, i, 0)),
        scratch_shapes=[pltpu.VMEM((halo + tm, CONV_DIM), F32)],
        compiler_params=_cparams(("arbitrary", "arbitrary"), est),
        name="conv_prompt",
    )(u, u, w, b.reshape(1, -1), g.reshape(1, -1), beta.reshape(1, -1))


def _masked_softmax_rows(s, mask):
    s = jnp.where(mask, s, NEG_INF)
    m = jnp.max(s, axis=1, keepdims=True)
    p = jnp.where(mask, jnp.exp(s - m), 0.0)
    l = jnp.sum(p, axis=1, keepdims=True)
    return p / jnp.where(l > 0.0, l, 1.0)


def _nsa_cmp_prompt_kernel(q_ref, kcvc_ref, oc_ref, imp_ref, *, tq):
    i = pl.program_id(1)
    nb = kcvc_ref.shape[0]
    kcvc = kcvc_ref[...]
    kc = kcvc[:, :NSA_DH].astype(BF16)
    vc = kcvc[:, NSA_DH:].astype(BF16)
    qpos = i * tq + lax.broadcasted_iota(jnp.int32, (tq, nb), 0)
    bend = (lax.broadcasted_iota(jnp.int32, (tq, nb), 1) + 1) * NSA_BLK - 1
    mask = bend <= qpos
    q = q_ref[...].astype(BF16)
    imp = jnp.zeros((tq, nb), F32)
    outs = []
    for h in range(NSA_HEADS):
        s = _dot_nt(q[:, h * NSA_DH:(h + 1) * NSA_DH], kc) * NSA_SCALE
        p = _masked_softmax_rows(s, mask)
        imp = imp + p
        outs.append(_dot(p.astype(BF16), vc))
    oc_ref[...] = jnp.concatenate(outs, axis=1)
    imp_ref[...] = imp


def _nsa_cmp_prompt(qn, kcvc, *, tq=256):
    nb = kcvc.shape[1]
    hd = NSA_HEADS * NSA_DH
    est = 2 * (tq * hd * 4 * 2 + nb * LANE * 4 + tq * LANE * 4) + 8 * tq * hd * 4
    return pl.pallas_call(
        functools.partial(_nsa_cmp_prompt_kernel, tq=tq),
        out_shape=(jax.ShapeDtypeStruct((BATCH, SEQ, hd), F32),
                   jax.ShapeDtypeStruct((BATCH, SEQ, nb), F32)),
        grid=(BATCH, SEQ // tq),
        in_specs=[pl.BlockSpec((None, tq, hd), lambda b, i: (b, i, 0)),
                  pl.BlockSpec((None, nb, 2 * NSA_DH), lambda b, i: (b, 0, 0))],
        out_specs=(pl.BlockSpec((None, tq, hd), lambda b, i: (b, i, 0)),
                   pl.BlockSpec((None, tq, nb), lambda b, i: (b, i, 0))),
        compiler_params=_cparams(("arbitrary", "arbitrary"), est),
        name="nsa_cmp_prompt",
    )(qn, kcvc)


def _nsa_cmp_sample_kernel(pt_ref, q_ref, pe_ref, w1_ref, w2_ref, *rest):
    del pt_ref
    pages = rest[:N_PAGES]
    oc_ref, imp_ref, x_sc = rest[N_PAGES:]
    nb = 2 * N_PAGES
    for p in range(N_PAGES):
        pg = pages[p][...]
        x_sc[pl.ds((2 * p) * CMP_PITCH, NSA_BLK), :] = pg[:NSA_BLK]
        x_sc[pl.ds((2 * p + 1) * CMP_PITCH, NSA_BLK), :] = pg[NSA_BLK:]
    acc = jnp.zeros((nb, 2 * NSA_PHI_HID), F32)
    for r2 in range(NSA_BLK // 2):
        a0 = x_sc[pl.ds(2 * r2, nb, stride=CMP_PITCH), :] + pe_ref[2 * r2:2 * r2 + 1, :]
        a1 = x_sc[pl.ds(2 * r2 + 1, nb, stride=CMP_PITCH), :] + pe_ref[2 * r2 + 1:2 * r2 + 2, :]
        a = jnp.concatenate([a0, a1], axis=1).astype(BF16)
        acc = acc + _dot(a, w1_ref[r2])
    hid = jax.nn.gelu(acc)
    kcvc = _dot(hid.astype(BF16), w2_ref[...])
    kc = kcvc[:, :NSA_DH].astype(BF16)
    vc = kcvc[:, NSA_DH:].astype(BF16)
    rows = DEC_SEQ * NSA_HEADS
    s = _dot_nt(q_ref[...], kc) * NSA_SCALE
    qpos = PAST_LEN + (lax.broadcasted_iota(jnp.int32, (rows, nb), 0) >> HEAD_SHIFT)
    bend = (lax.broadcasted_iota(jnp.int32, (rows, nb), 1) + 1) * NSA_BLK - 1
    p = _masked_softmax_rows(s, bend <= qpos)
    oc_ref[...] = _dot(p.astype(BF16), vc)
    imp_ref[...] = jnp.sum(p.reshape(DEC_SEQ, NSA_HEADS, nb), axis=1)


def _nsa_cmp_sample(page_table, qn, pe2, w1p, w2bd, cache, layer):
    rows = DEC_SEQ * NSA_HEADS
    nb = 2 * N_PAGES

    def page_spec(p):
        return pl.BlockSpec((None, None, PAGE_SIZE, 2 * NSA_DH), lambda b, pt: (pt[b, p], layer, 0, 0))

    est = (2 * N_PAGES * PAGE_SIZE * LANE * 4 + nb * CMP_PITCH * LANE * 4
           + 2 * (NSA_BLK // 2) * 2 * LANE * LANE * 2 + (8 << 20))
    grid_spec = pltpu.PrefetchScalarGridSpec(
        num_scalar_prefetch=1,
        grid=(DEC_BATCH,),
        in_specs=[pl.BlockSpec((None, rows, NSA_DH), lambda b, pt: (b, 0, 0)),
                  pl.BlockSpec((NSA_BLK, 2 * NSA_DH), lambda b, pt: (0, 0)),
                  pl.BlockSpec((NSA_BLK // 2, 4 * NSA_DH, 2 * NSA_PHI_HID), lambda b, pt: (0, 0, 0)),
                  pl.BlockSpec((2 * NSA_PHI_HID, 2 * NSA_DH), lambda b, pt: (0, 0))]
        + [page_spec(p) for p in range(N_PAGES)],
        out_specs=(pl.BlockSpec((None, rows, NSA_DH), lambda b, pt: (b, 0, 0)),
                   pl.BlockSpec((None, DEC_SEQ, nb), lambda b, pt: (b, 0, 0))),
        scratch_shapes=[pltpu.VMEM((nb * CMP_PITCH, 2 * NSA_DH), F32)],
    )
    return pl.pallas_call(
        _nsa_cmp_sample_kernel,
        out_shape=(jax.ShapeDtypeStruct((DEC_BATCH, rows, NSA_DH), F32),
                   jax.ShapeDtypeStruct((DEC_BATCH, DEC_SEQ, nb), F32)),
        grid_spec=grid_spec,
        compiler_params=_cparams(("arbitrary",), est),
        name="nsa_cmp_sample",
    )(page_table, qn, pe2, w1p, w2bd, *([cache] * N_PAGES))


def _split_heads(q):
    return jnp.concatenate([q[:, h * NSA_DH:(h + 1) * NSA_DH] for h in range(NSA_HEADS)], axis=0).astype(BF16)


def _merge_heads(o):
    return jnp.concatenate([o[h * Q_BLOCK:(h + 1) * Q_BLOCK] for h in range(NSA_HEADS)], axis=1)


def _nsa_sel_prompt_kernel(q_ref, kv_ref, sel_ref, o_ref, m_sc, l_sc, acc_sc, s_sc, p_sc, bias_sc, *, tk):
    qi = pl.program_id(1)
    kb = pl.program_id(2)
    rows = NSA_HEADS * Q_BLOCK
    last = ((qi + 1) * Q_BLOCK - 1) // tk
    nbs = sel_ref.shape[1]

    @pl.when(kb == 0)
    def _():
        _softmax_init(m_sc, l_sc, acc_sc)

    @pl.when(kb <= last)
    def _():
        q = _split_heads(q_ref[...])
        kv = kv_ref[...]
        k = kv[:, :NSA_DH].astype(BF16)
        v = kv[:, NSA_DH:].astype(BF16)
        kpos_e = kb * tk + lax.broadcasted_iota(jnp.int32, (nbs, tk), 1)
        expand = ((kpos_e >> BLK_SHIFT) == lax.broadcasted_iota(jnp.int32, (nbs, tk), 0)).astype(BF16)
        chosen = _dot(sel_ref[...].astype(BF16), expand)
        qpos = qi * Q_BLOCK + lax.broadcasted_iota(jnp.int32, (Q_BLOCK, tk), 0)
        kpos = kb * tk + lax.broadcasted_iota(jnp.int32, (Q_BLOCK, tk), 1)
        bias_sc[...] = jnp.where((chosen > 0.5) & (kpos <= qpos), 0.0, NEG_INF)
        s_sc[...] = _dot_nt(q, k)
        _flash_rows(s_sc, p_sc, bias_sc, m_sc, l_sc, acc_sc, scale=1.0,
                    n_groups=NSA_HEADS, group_rows=Q_BLOCK)
        acc_sc[...] += _dot(p_sc[...], v)

    @pl.when(kb == pl.num_programs(2) - 1)
    def _():
        o = acc_sc[...] / l_sc[...]
        o_ref[...] = _merge_heads(o)


def _nsa_sel_prompt(qr, kv_sel, selmask, *, tk=512):
    nq = SEQ // Q_BLOCK
    nk = SEQ // tk
    rows = NSA_HEADS * Q_BLOCK
    nbs = selmask.shape[2]

    def kv_map(b, qi, kb):
        return (b, jnp.minimum(kb, ((qi + 1) * Q_BLOCK - 1) // tk), 0)

    est = 2 * (rows * LANE * 2 + tk * LANE * 4 + Q_BLOCK * LANE * 4 + rows * LANE * 4) + 3 * rows * LANE * 4 \
        + 5 * rows * tk * 4
    return pl.pallas_call(
        functools.partial(_nsa_sel_prompt_kernel, tk=tk),
        out_shape=jax.ShapeDtypeStruct((BATCH, SEQ, NSA_HEADS * NSA_DH), F32),
        grid=(BATCH, nq, nk),
        in_specs=[pl.BlockSpec((None, Q_BLOCK, NSA_HEADS * NSA_DH), lambda b, qi, kb: (b, qi, 0)),
                  pl.BlockSpec((None, tk, 2 * NSA_DH), kv_map),
                  pl.BlockSpec((None, Q_BLOCK, nbs), lambda b, qi, kb: (b, qi, 0))],
        out_specs=pl.BlockSpec((None, Q_BLOCK, NSA_HEADS * NSA_DH), lambda b, qi, kb: (b, qi, 0)),
        scratch_shapes=[pltpu.VMEM((rows, 1), F32), pltpu.VMEM((rows, 1), F32),
                        pltpu.VMEM((rows, NSA_DH), F32), pltpu.VMEM((rows, tk), F32),
                        pltpu.VMEM((rows, tk), BF16), pltpu.VMEM((Q_BLOCK, tk), F32)],
        compiler_params=_cparams(("arbitrary", "arbitrary", "arbitrary"), est),
        name="nsa_sel_prompt",
    )(qr, kv_sel, selmask)


def _nsa_win_prompt_kernel(q_ref, *rest, nw):
    tiles = rest[:nw]
    o_ref, m_sc, l_sc, acc_sc, s_sc, p_sc, bias_sc = rest[nw:]
    qi = pl.program_id(1)
    rows = NSA_HEADS * Q_BLOCK
    nk = nw * Q_BLOCK
    _softmax_init(m_sc, l_sc, acc_sc)
    q = _split_heads(q_ref[...])
    kv = jnp.concatenate([t[...] for t in tiles], axis=0)
    k = kv[:, :NSA_DH].astype(BF16)
    v = kv[:, NSA_DH:].astype(BF16)
    qpos = qi * Q_BLOCK + lax.broadcasted_iota(jnp.int32, (Q_BLOCK, nk), 0)
    kpos = (qi - (nw - 1)) * Q_BLOCK + lax.broadcasted_iota(jnp.int32, (Q_BLOCK, nk), 1)
    bias_sc[...] = jnp.where((kpos <= qpos) & (kpos > qpos - NSA_WINDOW) & (kpos >= 0), 0.0, NEG_INF)
    s_sc[...] = _dot_nt(q, k)
    _flash_rows(s_sc, p_sc, bias_sc, m_sc, l_sc, acc_sc, scale=1.0, n_groups=NSA_HEADS, group_rows=Q_BLOCK)
    o = _dot(p_sc[...], v) / l_sc[...]
    o_ref[...] = _merge_heads(o)


def _nsa_win_prompt(qr, kv_win):
    nq = SEQ // Q_BLOCK
    nw = NSA_WINDOW // Q_BLOCK + 1
    rows = NSA_HEADS * Q_BLOCK
    nk = nw * Q_BLOCK

    def tile_spec(w):
        return pl.BlockSpec((None, Q_BLOCK, 2 * NSA_DH),
                            lambda b, qi: (b, jnp.maximum(qi - (nw - 1) + w, 0), 0))

    est = 2 * (rows * LANE * 2 + nk * LANE * 4 + rows * LANE * 4) + 3 * rows * LANE * 4 + 3 * rows * nk * 4
    return pl.pallas_call(
        functools.partial(_nsa_win_prompt_kernel, nw=nw),
        out_shape=jax.ShapeDtypeStruct((BATCH, SEQ, NSA_HEADS * NSA_DH), F32),
        grid=(BATCH, nq),
        in_specs=[pl.BlockSpec((None, Q_BLOCK, NSA_HEADS * NSA_DH), lambda b, qi: (b, qi, 0))]
        + [tile_spec(w) for w in range(nw)],
        out_specs=pl.BlockSpec((None, Q_BLOCK, NSA_HEADS * NSA_DH), lambda b, qi: (b, qi, 0)),
        scratch_shapes=[pltpu.VMEM((rows, 1), F32), pltpu.VMEM((rows, 1), F32),
                        pltpu.VMEM((rows, NSA_DH), F32), pltpu.VMEM((rows, nk), F32),
                        pltpu.VMEM((rows, nk), BF16), pltpu.VMEM((Q_BLOCK, nk), F32)],
        compiler_params=_cparams(("arbitrary", "arbitrary"), est),
        name="nsa_win_prompt",
    )(qr, *([kv_win] * nw))


def _nsa_selwin_sample_kernel(pt_ref, q_ref, allowed_ref, selnew_ref, win_ref, winnew_ref, *rest):
    del pt_ref
    pages = rest[:N_PAGES]
    os_ref, ow_ref = rest[N_PAGES:]
    rows = DEC_SEQ * NSA_HEADS
    q = q_ref[...]
    kv = jnp.concatenate([p[...] for p in pages] + [selnew_ref[...]], axis=0)
    nk = kv.shape[0]
    k = kv[:, :NSA_DH].astype(BF16)
    v = kv[:, NSA_DH:].astype(BF16)
    allowed = jnp.broadcast_to(allowed_ref[...][:, None, :], (DEC_SEQ, NSA_HEADS, nk)).reshape(rows, nk)
    p = _masked_softmax_rows(_dot_nt(q, k), allowed > 0.5)
    os_ref[...] = _dot(p.astype(BF16), v)
    kvw = jnp.concatenate([win_ref[...], winnew_ref[...]], axis=0)
    nkw = kvw.shape[0]
    wb = win_ref.shape[0]
    kw = kvw[:, :NSA_DH].astype(BF16)
    vw = kvw[:, NSA_DH:].astype(BF16)
    t = lax.broadcasted_iota(jnp.int32, (rows, nkw), 0) >> HEAD_SHIFT
    j = lax.broadcasted_iota(jnp.int32, (rows, nkw), 1)
    kp = PAST_LEN - wb + j
    qp = PAST_LEN + t
    wmask = (kp <= qp) & (kp > qp - NSA_WINDOW) & (kp >= 0) & (j < wb + DEC_SEQ)
    pw = _masked_softmax_rows(_dot_nt(q, kw), wmask)
    ow_ref[...] = _dot(pw.astype(BF16), vw)


def _nsa_selwin_sample(page_table, qr, allowed, sel_new, win_state, win_new, cache, layer):
    rows = DEC_SEQ * NSA_HEADS
    nk = PAST_LEN + PAGE_SIZE
    wb = win_state.shape[2]

    def page_spec(p):
        return pl.BlockSpec((None, None, PAGE_SIZE, 2 * NSA_DH), lambda b, pt: (pt[b, p], layer, 0, 0))

    est = 2 * (N_PAGES + 2) * PAGE_SIZE * LANE * 4 + 2 * wb * LANE * 4 + 3 * nk * LANE * 4 + 8 * rows * nk * 4
    grid_spec = pltpu.PrefetchScalarGridSpec(
        num_scalar_prefetch=1,
        grid=(DEC_BATCH,),
        in_specs=[pl.BlockSpec((None, rows, NSA_DH), lambda b, pt: (b, 0, 0)),
                  pl.BlockSpec((None, DEC_SEQ, nk), lambda b, pt: (b, 0, 0)),
                  pl.BlockSpec((None, PAGE_SIZE, 2 * NSA_DH), lambda b, pt: (b, 0, 0)),
                  pl.BlockSpec((None, None, wb, 2 * NSA_DH), lambda b, pt: (b, layer, 0, 0)),
                  pl.BlockSpec((None, PAGE_SIZE, 2 * NSA_DH), lambda b, pt: (b, 0, 0))]
        + [page_spec(p) for p in range(N_PAGES)],
        out_specs=(pl.BlockSpec((None, rows, NSA_DH), lambda b, pt: (b, 0, 0)),
                   pl.BlockSpec((None, rows, NSA_DH), lambda b, pt: (b, 0, 0))),
    )
    return pl.pallas_call(
        _nsa_selwin_sample_kernel,
        out_shape=(jax.ShapeDtypeStruct((DEC_BATCH, rows, NSA_DH), F32),
                   jax.ShapeDtypeStruct((DEC_BATCH, rows, NSA_DH), F32)),
        grid_spec=grid_spec,
        compiler_params=_cparams(("arbitrary",), est),
        name="nsa_selwin_sample",
    )(page_table, qr, allowed, sel_new, win_state, win_new, *([cache] * N_PAGES))


def _moe_row_copy(x_hbm, xbuf, sem, tok, slot, r):
    return pltpu.make_async_copy(x_hbm.at[pl.ds(tok, 1)], xbuf.at[slot, pl.ds(r, 1)], sem.at[slot])


def _moe_kernel(be_ref, nu_ref, tok0_ref, tokn_ref, x_hbm, sw_ref, wgu_ref, bgu_ref, wdn_ref, bdn_ref,
                o_ref, xbuf, sem):
    del be_ref
    i = pl.program_id(0)
    n_used = nu_ref[0]

    def start_rows(tok_ref, slot):
        def body(r, carry):
            _moe_row_copy(x_hbm, xbuf, sem, tok_ref[0, r], slot, r).start()
            return carry
        lax.fori_loop(0, MOE_TM, body, 0, unroll=8)

    def wait_rows(slot):
        def body(r, carry):
            _moe_row_copy(x_hbm, xbuf, sem, 0, slot, r).wait()
            return carry
        lax.fori_loop(0, MOE_TM, body, 0, unroll=8)

    @pl.when(i == 0)
    def _():
        start_rows(tok0_ref, 0)

    @pl.when(i + 1 < n_used)
    def _():
        start_rows(tokn_ref, (i + 1) % 2)

    @pl.when(i < n_used)
    def _():
        slot = i % 2
        wait_rows(slot)
        h = _dot(xbuf[slot].astype(BF16), wgu_ref[...]) + bgu_ref[...]
        gate = jnp.minimum(h[:, :D_FF], SWIGLU_LIMIT)
        up = jnp.clip(h[:, D_FF:], -SWIGLU_LIMIT, SWIGLU_LIMIT)
        act = gate * jax.nn.sigmoid(SWIGLU_ALPHA * gate) * (up + 1.0)
        y = _dot(act.astype(BF16), wdn_ref[...]) + bdn_ref[...]
        o_ref[...] = y * sw_ref[...]

    @pl.when(i >= n_used)
    def _():
        o_ref[...] = jnp.zeros(o_ref.shape, F32)


def _moe_experts(blk_e, n_used, slot_tok, x, slot_w, w_gu, b_gu, w_dn, b_dn):
    tm = MOE_TM

    def row_map(i, be, nu):
        return (jnp.minimum(i, nu[0] - 1), 0)

    def exp_map(i, be, nu):
        return (be[jnp.minimum(i, nu[0] - 1)], 0, 0)

    est = 2 * (tm * LANE * 4 + D_MODEL * 2 * D_FF * 2 + D_FF * D_MODEL * 2 + tm * D_MODEL * 4) \
        + 2 * tm * D_MODEL * 4 + tm * D_MODEL * 2 + 6 * tm * 2 * D_FF * 4
    grid_spec = pltpu.PrefetchScalarGridSpec(
        num_scalar_prefetch=2,
        grid=(MOE_BLOCKS,),
        in_specs=[pl.BlockSpec((None, 1, tm), lambda i, be, nu: (0, 0, 0), memory_space=pltpu.SMEM),
                  pl.BlockSpec((None, 1, tm), lambda i, be, nu: (jnp.minimum(i + 1, MOE_BLOCKS - 1), 0, 0),
                               memory_space=pltpu.SMEM),
                  pl.BlockSpec(memory_space=pl.ANY),
                  pl.BlockSpec((tm, 1), row_map),
                  pl.BlockSpec((None, D_MODEL, 2 * D_FF), exp_map),
                  pl.BlockSpec((None, 1, 2 * D_FF), exp_map),
                  pl.BlockSpec((None, D_FF, D_MODEL), exp_map),
                  pl.BlockSpec((None, 1, D_MODEL), exp_map)],
        out_specs=pl.BlockSpec((tm, D_MODEL), lambda i, be, nu: (i, 0)),
        scratch_shapes=[pltpu.VMEM((2, tm, D_MODEL), F32), pltpu.SemaphoreType.DMA((2,))],
    )
    return pl.pallas_call(
        _moe_kernel,
        out_shape=jax.ShapeDtypeStruct((MOE_SLOTS, D_MODEL), F32),
        grid_spec=grid_spec,
        compiler_params=_cparams(("arbitrary",), est),
        name="moe_experts",
    )(blk_e, n_used, slot_tok, slot_tok, x, slot_w, w_gu, b_gu, w_dn, b_dn)


def _rms_norm(x, g):
    return x * lax.rsqrt(jnp.mean(x * x, axis=-1, keepdims=True) + RMS_EPS) * g


def _layer_norm(x, g, b):
    xc = x - jnp.mean(x, axis=-1, keepdims=True)
    var = jnp.mean(xc * xc, axis=-1, keepdims=True)
    return xc * lax.rsqrt(var + LN_EPS) * g + b


def _rope(x, pos, rot_dim):
    half = rot_dim // 2
    freqs = ROPE_THETA ** (-jnp.arange(half, dtype=F32) * (2.0 / rot_dim))
    ang = pos.astype(F32)[:, None] * freqs[None, :]
    shape = (ang.shape[0],) + (1,) * (x.ndim - 2) + (half,)
    cos = jnp.cos(ang).reshape(shape)
    sin = jnp.sin(ang).reshape(shape)
    x1, x2, rest = x[..., :half], x[..., half:rot_dim], x[..., rot_dim:]
    return jnp.concatenate([x1 * cos - x2 * sin, x2 * cos + x1 * sin, rest], axis=-1)


def _pad_rows(x, n):
    return jnp.pad(x, ((0, 0), (0, n - x.shape[1]), (0, 0)))


def _select_blocks(imp, q_pos, nbs):
    nb = imp.shape[-1]
    imp = jnp.pad(imp, ((0, 0), (0, 0), (0, nbs - nb)), constant_values=-1.0)
    blk = jnp.arange(nbs)[None, :]
    cur = (q_pos // NSA_BLK)[:, None]
    forced = (blk == 0) | (blk == cur) | (blk == cur - 1)
    score = jnp.where(blk > cur, -1.0, jnp.where(forced, FORCE_SCORE, imp))
    n_sel = min(NSA_TOPN, nbs)
    top_s, top_i = lax.top_k(score, n_sel)
    onehot = (top_i[..., None] == jnp.arange(nbs)) & (top_s >= 0.0)[..., None]
    return jnp.any(onehot, axis=-2).astype(F32)


def _moe(x, router_w, router_b, w_gu, b_gu, w_dn, b_dn, layer):
    n_tok = x.shape[0]
    n_asg = n_tok * TOP_K
    logits = _mm_f32(x, router_w, tm=512, name=f"router_l{layer}") + router_b
    top_v, top_e = lax.top_k(logits, TOP_K)
    gates = jax.nn.softmax(top_v, axis=-1)
    flat_e = top_e.reshape(-1)
    onehot = (flat_e[:, None] == jnp.arange(N_EXPERTS)[None, :]).astype(jnp.int32)
    running = jnp.cumsum(onehot, axis=0)
    rank = jnp.sum((running - 1) * onehot, axis=1)
    counts = running[-1]
    padded = (counts + MOE_TM - 1) // MOE_TM * MOE_TM
    pad_end = jnp.cumsum(padded)
    pad_start = pad_end - padded
    dest = (pad_start[flat_e] + rank).astype(jnp.int32)
    tok_of = (jnp.arange(n_asg) // TOP_K).astype(jnp.int32)
    slot_tok = jnp.zeros((MOE_SLOTS,), jnp.int32).at[dest].set(tok_of)
    slot_w = jnp.zeros((MOE_SLOTS,), F32).at[dest].set(gates.reshape(-1))
    blk_e = jnp.minimum(jnp.searchsorted(pad_end, jnp.arange(MOE_BLOCKS) * MOE_TM, side='right'),
                        N_EXPERTS - 1).astype(jnp.int32)
    n_used = (pad_end[-1] // MOE_TM).astype(jnp.int32).reshape(1)
    ys = _moe_experts(blk_e, n_used, slot_tok.reshape(MOE_BLOCKS, 1, MOE_TM), x, slot_w[:, None],
                      w_gu, b_gu[:, None, :], w_dn, b_dn[:, None, :])
    return jnp.sum(ys[dest.reshape(n_tok, TOP_K)], axis=1)


def _prep_weights(w_in, mla_w_uq, mla_w_uk, mla_w_uv, nsa_phi_w1, nsa_phi_w2):
    d = w_in.shape[0]
    z = lambda n: jnp.zeros((d, D_MODEL, n), F32)
    w_in_p = jnp.concatenate([w_in[:, :, :672], z(96), w_in[:, :, 672:4176], z(80), w_in[:, :, 4176:],
                              z(IN_PAD_N - IN_GM - 3 * D_MODEL)], axis=2).astype(BF16)
    uq = mla_w_uq.reshape(d, MLA_Q_LORA, MLA_HEADS, MLA_NOPE + MLA_ROPE)
    w_uq_p = jnp.concatenate([uq[..., :MLA_NOPE].reshape(d, MLA_Q_LORA, -1),
                              uq[..., MLA_NOPE:].reshape(d, MLA_Q_LORA, -1)], axis=2).astype(BF16)
    w_uk_t = jnp.transpose(mla_w_uk, (0, 2, 3, 1)).astype(BF16)
    w_uv_t = jnp.transpose(mla_w_uv, (0, 2, 1, 3)).astype(BF16)
    w1 = nsa_phi_w1.reshape(d, 2, NSA_BLK, NSA_DH, NSA_PHI_HID)
    zero = jnp.zeros_like(w1[:, 0])
    w1k = jnp.concatenate([w1[:, 0], zero], axis=-1)
    w1v = jnp.concatenate([zero, w1[:, 1]], axis=-1)
    w1r = jnp.concatenate([w1k, w1v], axis=2)
    w1_flat = w1r.reshape(d, NSA_BLK * 2 * NSA_DH, 2 * NSA_PHI_HID).astype(BF16)
    w1_pair = w1r.reshape(d, NSA_BLK // 2, 4 * NSA_DH, 2 * NSA_PHI_HID).astype(BF16)
    z2 = jnp.zeros((d, NSA_PHI_HID, NSA_DH), F32)
    w2bd = jnp.concatenate([jnp.concatenate([nsa_phi_w2[:, 0], z2], axis=2),
                            jnp.concatenate([z2, nsa_phi_w2[:, 1]], axis=2)], axis=1).astype(BF16)
    return w_in_p, w_uq_p, w_uk_t, w_uv_t, w1_flat, w1_pair, w2bd


def kernel(x_prompt, x_sample, cache_mla, cache_nsa_cmp, cache_nsa_sel, state_nsa_win, state_conv, page_table,
           w_in, mla_q_norm, mla_kv_norm, mla_w_uq, mla_w_uk, mla_w_uv, mla_w_br,
           conv_w, conv_b, conv_ln_g, conv_ln_b, conv_w_br,
           nsa_phi_pe, nsa_phi_w1, nsa_phi_w2, nsa_w_br, w_out,
           ln1_g, ln1_b, ln2_g, ln2_b, router_w, router_b, moe_w_gu, moe_b_gu, moe_w_dn, moe_b_dn):
    w_in_p, w_uq_p, w_uk_t, w_uv_t, w1_flat, w1_pair, w2bd = _prep_weights(
        w_in, mla_w_uq, mla_w_uk, mla_w_uv, nsa_phi_w1, nsa_phi_w2)
    pos = jnp.concatenate([jnp.tile(jnp.arange(SEQ), BATCH),
                           jnp.tile(PAST_LEN + jnp.arange(DEC_SEQ), DEC_BATCH)])
    q_pos_p = jnp.arange(SEQ)
    q_pos_s = PAST_LEN + jnp.arange(DEC_SEQ)
    nbs_s = -(-(PAST_LEN + DEC_SEQ) // NSA_BLK)
    mp = M_PROMPT
    x = jnp.concatenate([x_prompt.reshape(mp, D_MODEL), x_sample.reshape(M_SAMPLE, D_MODEL)], axis=0)
    cache_mla_t = jnp.swapaxes(cache_mla, 2, 3)
    st_p, st_s = [], []
    for l in range(DEPTH):
        h = _mm(x, w_in_p[l], tm=512, tn=1536, name=f"in_proj_l{l}")
        cq = _rms_norm(h[:, IN_CQ:IN_CQ + MLA_Q_LORA], mla_q_norm[l])
        q = _mm(cq, w_uq_p[l], tm=1088, tn=1536, name=f"mla_uq_l{l}")
        q_lat = _head_up(q[:, :MLA_HEADS * MLA_NOPE], w_uk_t[l], tm=1088, out_dtype=BF16, name=f"mla_uk_l{l}")
        q_rope = _rope(q[:, MLA_HEADS * MLA_NOPE:].reshape(M_ALL, MLA_HEADS, MLA_ROPE), pos, MLA_ROPE)
        q_rope = jnp.transpose(q_rope, (1, 0, 2)).astype(BF16)
        mla_rows = jnp.concatenate([_rms_norm(h[:, IN_CKV:IN_CKV + MLA_KV_LORA], mla_kv_norm[l]),
                                    _rope(h[:, IN_KR:IN_KR + MLA_ROPE], pos, MLA_ROPE)], axis=1)
        mla_rows_p = mla_rows[:mp].reshape(BATCH, SEQ, MLA_CACHE)
        mla_rows_s = mla_rows[mp:].reshape(DEC_BATCH, DEC_SEQ, MLA_CACHE)
        o_lat_p = _mla_prompt(q_lat[:, :mp].reshape(MLA_HEADS, BATCH, SEQ, MLA_KV_LORA),
                              q_rope[:, :mp].reshape(MLA_HEADS, BATCH, SEQ, MLA_ROPE), mla_rows_p)

        def to_rows(a):
            dd = a.shape[-1]
            return jnp.transpose(a.reshape(MLA_HEADS, DEC_BATCH, DEC_SEQ, dd), (1, 2, 0, 3)).reshape(
                DEC_BATCH, DEC_SEQ * MLA_HEADS, dd)

        o_lat_s = _mla_sample(page_table, to_rows(q_lat[:, mp:]), to_rows(q_rope[:, mp:]),
                              jnp.swapaxes(_pad_rows(mla_rows_s, PAGE_SIZE), 1, 2), cache_mla_t, l)
        o_lat_s = jnp.transpose(o_lat_s.reshape(DEC_BATCH, DEC_SEQ, MLA_HEADS, MLA_KV_LORA),
                                (2, 0, 1, 3)).reshape(MLA_HEADS, M_SAMPLE, MLA_KV_LORA)
        o_lat = jnp.concatenate([o_lat_p.reshape(MLA_HEADS, mp, MLA_KV_LORA), o_lat_s], axis=1)
        o_mla = _head_down(o_lat, w_uv_t[l], tm=1088, out_dtype=BF16, name=f"mla_uv_l{l}")
        y_a = _mm(o_mla, mla_w_br[l].astype(BF16), tm=1088, tn=1024, name=f"mla_br_l{l}")
        glu = h[:, IN_GLU:IN_GLU + 2 * CONV_DIM]
        u = glu[:, :CONV_DIM] * jax.nn.sigmoid(glu[:, CONV_DIM:])
        u_p = u[:mp].reshape(BATCH, SEQ, CONV_DIM)
        u_s = u[mp:].reshape(DEC_BATCH, DEC_SEQ, CONV_DIM)
        c_p = _conv_prompt(u_p, conv_w[l], conv_b[l], conv_ln_g[l], conv_ln_b[l])
        u_ext = jnp.concatenate([state_conv[:, l], u_s], axis=1)
        y_s = sum(u_ext[:, k:k + DEC_SEQ] * conv_w[l][k] for k in range(CONV_WIDTH)) + conv_b[l]
        c_s = jax.nn.silu(_layer_norm(y_s, conv_ln_g[l], conv_ln_b[l])).astype(BF16)
        c_all = jnp.concatenate([c_p.reshape(mp, CONV_DIM), c_s.reshape(M_SAMPLE, CONV_DIM)], axis=0)
        y_conv = _mm(c_all, conv_w_br[l].astype(BF16), tm=1088, tn=1024, name=f"conv_br_l{l}")
        qn = h[:, IN_QN:IN_QN + NSA_HEADS * NSA_DH]
        qr = _rope(qn.reshape(M_ALL, NSA_HEADS, NSA_DH), pos, NSA_ROT) * NSA_SCALE
        kv_cmp = h[:, IN_CMP:IN_CMP + 2 * NSA_DH]
        kv_sel = h[:, IN_SEL:IN_SEL + 2 * NSA_DH]
        kv_sel = jnp.concatenate([_rope(kv_sel[:, :NSA_DH], pos, NSA_ROT), kv_sel[:, NSA_DH:]], axis=1)
        kv_win = h[:, IN_WIN:IN_WIN + 2 * NSA_DH]
        kv_win = jnp.concatenate([_rope(kv_win[:, :NSA_DH], pos, NSA_ROT), kv_win[:, NSA_DH:]], axis=1)
        g_nsa = jax.nn.sigmoid(h[:, IN_GN:IN_GN + 3 * NSA_HEADS]).reshape(M_ALL, NSA_HEADS, 3)
        pe2 = nsa_phi_pe[l].reshape(NSA_BLK, 2 * NSA_DH)
        kv_cmp_p = kv_cmp[:mp].reshape(BATCH, SEQ, 2 * NSA_DH)
        nb_p = SEQ // NSA_BLK
        flat = (kv_cmp_p.reshape(BATCH, nb_p, NSA_BLK, 2 * NSA_DH) + pe2).reshape(BATCH * nb_p, -1)
        hid = jax.nn.gelu(_mm(flat, w1_flat[l], tm=BATCH * nb_p, tn=2 * NSA_PHI_HID, name=f"nsa_phi1_l{l}"))
        kcvc_p = _mm(hid, w2bd[l], tm=BATCH * nb_p, tn=2 * NSA_DH, name=f"nsa_phi2_l{l}")
        o_c_p, imp_p = _nsa_cmp_prompt(qn[:mp].reshape(BATCH, SEQ, -1), kcvc_p.reshape(BATCH, nb_p, 2 * NSA_DH))
        selmask_p = _select_blocks(imp_p, q_pos_p, nb_p)
        qr_p = qr[:mp].reshape(BATCH, SEQ, NSA_HEADS * NSA_DH)
        kv_sel_p = kv_sel[:mp].reshape(BATCH, SEQ, 2 * NSA_DH)
        kv_win_p = kv_win[:mp].reshape(BATCH, SEQ, 2 * NSA_DH)
        o_s_p = _nsa_sel_prompt(qr_p, kv_sel_p, selmask_p)
        o_w_p = _nsa_win_prompt(qr_p, kv_win_p)
        to_tok = lambda a: a.reshape(mp, NSA_HEADS, NSA_DH)
        o_c_p, o_s_p, o_w_p = to_tok(o_c_p), to_tok(o_s_p), to_tok(o_w_p)
        kv_sel_s = kv_sel[mp:].reshape(DEC_BATCH, DEC_SEQ, 2 * NSA_DH)
        kv_win_s = kv_win[mp:].reshape(DEC_BATCH, DEC_SEQ, 2 * NSA_DH)
        qn_s = qn[mp:].reshape(DEC_BATCH, DEC_SEQ * NSA_HEADS, NSA_DH).astype(BF16)
        qr_s = qr[mp:].reshape(DEC_BATCH, DEC_SEQ * NSA_HEADS, NSA_DH).astype(BF16)
        o_c_s, imp_s = _nsa_cmp_sample(page_table, qn_s, pe2, w1_pair[l], w2bd[l], cache_nsa_cmp, l)
        selmask_s = _select_blocks(imp_s, q_pos_s, nbs_s)
        key_pos = jnp.arange(PAST_LEN + PAGE_SIZE)
        allowed = jnp.repeat(selmask_s[:, :, :nbs_s - 1], NSA_BLK, axis=2)
        new_ok = selmask_s[:, :, nbs_s - 1:] * (key_pos[None, None, PAST_LEN:] <= q_pos_s[None, :, None])
        allowed = jnp.concatenate([allowed, new_ok.astype(F32)], axis=2)
        o_s_s, o_w_s = _nsa_selwin_sample(page_table, qr_s, allowed, _pad_rows(kv_sel_s, PAGE_SIZE),
                                          state_nsa_win, _pad_rows(kv_win_s, PAGE_SIZE), cache_nsa_sel, l)
        cat = lambda a, b: jnp.concatenate([a, b.reshape(M_SAMPLE, NSA_HEADS, NSA_DH)], axis=0)
        o_nsa = (g_nsa[..., 0:1] * cat(o_c_p, o_c_s) + g_nsa[..., 1:2] * cat(o_s_p, o_s_s)
                 + g_nsa[..., 2:3] * cat(o_w_p, o_w_s)).reshape(M_ALL, NSA_HEADS * NSA_DH)
        y_n = _mm(o_nsa, nsa_w_br[l].astype(BF16), tm=1088, tn=1024, name=f"nsa_br_l{l}")
        gm = jax.nn.sigmoid(h[:, IN_GM:IN_GM + 3 * D_MODEL])
        m = gm[:, :D_MODEL] * y_a + gm[:, D_MODEL:2 * D_MODEL] * y_conv + gm[:, 2 * D_MODEL:] * y_n
        mix = _mm(m, w_out[l].astype(BF16), tm=1088, tn=1024, name=f"w_out_l{l}")
        x1 = _layer_norm(DN_ALPHA * x + mix, ln1_g[l], ln1_b[l])
        f = _moe(x1, router_w[l], router_b[l], moe_w_gu[l].astype(BF16), moe_b_gu[l],
                 moe_w_dn[l].astype(BF16), moe_b_dn[l], l)
        x = _layer_norm(DN_ALPHA * x1 + f, ln2_g[l], ln2_b[l])
        win_all = jnp.concatenate([state_nsa_win[:, l], kv_win_s], axis=1)
        st_p.append((mla_rows_p, kv_cmp_p, kv_sel_p, kv_win_p[:, -min(NSA_WINDOW, SEQ):],
                     u_p[:, -(CONV_WIDTH - 1):]))
        st_s.append((mla_rows_s, kv_cmp[mp:].reshape(DEC_BATCH, DEC_SEQ, 2 * NSA_DH), kv_sel_s,
                     win_all[:, -state_nsa_win.shape[2]:], u_ext[:, -(CONV_WIDTH - 1):]))
    stack = lambda sts, i: jnp.stack([s[i] for s in sts], axis=1)
    return (x[:mp].reshape(BATCH, SEQ, D_MODEL), x[mp:].reshape(DEC_BATCH, DEC_SEQ, D_MODEL),
            stack(st_p, 0), stack(st_p, 1), stack(st_p, 2), stack(st_p, 3), stack(st_p, 4),
            stack(st_s, 0), stack(st_s, 1), stack(st_s, 2), stack(st_s, 3), stack(st_s, 4))
```

```python
import functools

import numpy as np
import jax
import jax.numpy as jnp
from jax import lax
from jax.experimental import pallas as pl
from jax.experimental.pallas import tpu as pltpu

F32 = jnp.float32
BF16 = jnp.bfloat16

D_MODEL = 2048
BATCH = 2
SEQ = 4096
DEPTH = 4
DEC_BATCH = 128
DEC_SEQ = 4
PAST_LEN = 8192
PAGE_SIZE = 128
N_PAGES = PAST_LEN // PAGE_SIZE

MLA_HEADS = 16
MLA_NOPE = 64
MLA_ROPE = 32
MLA_V = 64
MLA_Q_LORA = 384
MLA_KV_LORA = 256
MLA_CACHE = MLA_KV_LORA + MLA_ROPE
MLA_SCALE = (MLA_NOPE + MLA_ROPE) ** -0.5
CONV_DIM = 1024
CONV_WIDTH = 31
NSA_HEADS = 16
NSA_DH = 64
NSA_ROT = NSA_DH // 4
NSA_BLK = 64
NSA_TOPN = 16
NSA_WINDOW = 512
NSA_PHI_HID = 64
NSA_SCALE = NSA_DH ** -0.5
N_EXPERTS = 32
TOP_K = 4
D_FF = 768
SWIGLU_LIMIT = 7.0
SWIGLU_ALPHA = 1.702
ROPE_THETA = 500000.0
Q_BLOCK = 128
DN_ALPHA = (2 * DEPTH) ** 0.25
NEG_INF = -1e30
FORCE_SCORE = 1e6
RMS_EPS = 1e-6
LN_EPS = 1e-5

M_PROMPT = BATCH * SEQ
M_SAMPLE = DEC_BATCH * DEC_SEQ
M_ALL = M_PROMPT + M_SAMPLE

V7X_VMEM_LIMIT_CAP = 60000 * 1024
LANE = 128

IN_CQ = 0
IN_CKV = 384
IN_KR = 640
IN_GLU = 768
IN_QN = 2816
IN_CMP = 3840
IN_SEL = 3968
IN_WIN = 4096
IN_GN = 4224
IN_GM = 4352
IN_PAD_N = 10752

MOE_TM = 256
MOE_BLOCKS = (M_ALL * TOP_K) // MOE_TM + N_EXPERTS
MOE_SLOTS = MOE_BLOCKS * MOE_TM

CMP_PITCH = 72
HEAD_SHIFT = 4
BLK_SHIFT = 6
assert (1 << HEAD_SHIFT) == MLA_HEADS == NSA_HEADS and (1 << BLK_SHIFT) == NSA_BLK
assert NSA_SCALE == 2.0 ** -3


def _cparams(sem, est_bytes):
    limit = int(min(max(est_bytes * 5 // 4 + (4 << 20), 32 << 20), V7X_VMEM_LIMIT_CAP))
    return pltpu.CompilerParams(dimension_semantics=sem, vmem_limit_bytes=limit)


def _dot(a, b):
    return jnp.dot(a, b, preferred_element_type=F32)


def _dot_nt(a, b):
    return lax.dot_general(a, b, (((1,), (1,)), ((), ())), preferred_element_type=F32)


def _mm_kernel(x_ref, w_ref, o_ref):
    o_ref[...] = _dot(x_ref[...].astype(BF16), w_ref[...].astype(BF16)).astype(o_ref.dtype)


def _mm(x, w, *, tm, tn, out_dtype=F32, name):
    m, k = x.shape
    n = w.shape[1]
    est = 2 * (tm * k * x.dtype.itemsize + k * tn * w.dtype.itemsize + tm * tn * 4) + tm * k * 2 + k * tn * 2
    return pl.pallas_call(
        _mm_kernel,
        out_shape=jax.ShapeDtypeStruct((m, n), out_dtype),
        grid=(pl.cdiv(n, tn), pl.cdiv(m, tm)),
        in_specs=[pl.BlockSpec((tm, k), lambda j, i: (i, 0)),
                  pl.BlockSpec((k, tn), lambda j, i: (0, j))],
        out_specs=pl.BlockSpec((tm, tn), lambda j, i: (i, j)),
        compiler_params=_cparams(("arbitrary", "arbitrary"), est),
        name=name,
    )(x, w)


def _mm_f32_kernel(x_ref, w_ref, o_ref):
    o_ref[...] = jnp.dot(x_ref[...], w_ref[...], preferred_element_type=F32,
                         precision=lax.Precision.HIGHEST)


def _mm_f32(x, w, *, tm, name):
    m, k = x.shape
    n = w.shape[1]
    est = 2 * (tm * k * 4 + k * LANE * 4 + tm * LANE * 4)
    return pl.pallas_call(
        _mm_f32_kernel,
        out_shape=jax.ShapeDtypeStruct((m, n), F32),
        grid=(pl.cdiv(m, tm),),
        in_specs=[pl.BlockSpec((tm, k), lambda i: (i, 0)),
                  pl.BlockSpec((k, n), lambda i: (0, 0))],
        out_specs=pl.BlockSpec((tm, n), lambda i: (i, 0)),
        compiler_params=_cparams(("arbitrary",), est),
        name=name,
    )(x, w)


def _head_up_kernel(x_ref, w_ref, o_ref, *, scale):
    kh = w_ref.shape[1]
    x = x_ref[...].astype(BF16)
    for j in range(2):
        o_ref[j] = (_dot(x[:, j * kh:(j + 1) * kh], w_ref[j]) * scale).astype(o_ref.dtype)


def _head_up(x, w, *, tm, out_dtype, name, scale=1.0):
    m = x.shape[0]
    nheads, kh, nh = w.shape
    est = 2 * (tm * 2 * kh * 4 + 2 * kh * nh * 2 + 2 * tm * nh * 4)
    return pl.pallas_call(
        functools.partial(_head_up_kernel, scale=scale),
        out_shape=jax.ShapeDtypeStruct((nheads, m, nh), out_dtype),
        grid=(nheads // 2, pl.cdiv(m, tm)),
        in_specs=[pl.BlockSpec((tm, 2 * kh), lambda h, i: (i, h)),
                  pl.BlockSpec((2, kh, nh), lambda h, i: (h, 0, 0))],
        out_specs=pl.BlockSpec((2, tm, nh), lambda h, i: (h, i, 0)),
        compiler_params=_cparams(("arbitrary", "arbitrary"), est),
        name=name,
    )(x, w)


def _head_down_kernel(x_ref, w_ref, o_ref):
    o_ref[...] = jnp.concatenate(
        [_dot(x_ref[j].astype(BF16), w_ref[j]) for j in range(2)], axis=1).astype(o_ref.dtype)


def _head_down(x, w, *, tm, out_dtype, name):
    nheads, m, kh = x.shape
    nh = w.shape[2]
    est = 2 * (2 * tm * kh * 4 + 2 * kh * nh * 2 + tm * 2 * nh * 4)
    return pl.pallas_call(
        _head_down_kernel,
        out_shape=jax.ShapeDtypeStruct((m, nheads * nh), out_dtype),
        grid=(nheads // 2, pl.cdiv(m, tm)),
        in_specs=[pl.BlockSpec((2, tm, kh), lambda h, i: (h, i, 0)),
                  pl.BlockSpec((2, kh, nh), lambda h, i: (h, 0, 0))],
        out_specs=pl.BlockSpec((tm, 2 * nh), lambda h, i: (i, h)),
        compiler_params=_cparams(("arbitrary", "arbitrary"), est),
        name=name,
    )(x, w)


def _softmax_init(m_sc, l_sc, acc_sc):
    m_sc[...] = jnp.full(m_sc.shape, NEG_INF, F32)
    l_sc[...] = jnp.zeros(l_sc.shape, F32)
    acc_sc[...] = jnp.zeros(acc_sc.shape, F32)


def _softmax_update(s, v, m_sc, l_sc, acc_sc, *, v_transposed=False):
    m_prev = m_sc[...]
    m_new = jnp.maximum(m_prev, jnp.max(s, axis=1, keepdims=True))
    alpha = jnp.exp(m_prev - m_new)
    p = jnp.exp(s - m_new)
    pb = p.astype(BF16)
    pv = _dot_nt(pb, v) if v_transposed else _dot(pb, v)
    l_sc[...] = alpha * l_sc[...] + jnp.sum(p, axis=1, keepdims=True)
    acc_sc[...] = alpha * acc_sc[...] + pv
    m_sc[...] = m_new


def _lanes(x, n):
    return x[:, :n] if n <= LANE else jnp.tile(x, (1, n // LANE))


def _flash_rows(s_sc, p_sc, bias, m_sc, l_sc, acc_sc, *, group_rows, sub):
    rows, keys = s_sc.shape
    dv = acc_sc.shape[1]

    def step(i, carry):
        rs = pl.ds(pl.multiple_of(i * sub, sub), sub)
        s = s_sc[rs, :]
        if bias is not None:
            s = _add_group_bias(s, bias, sub // group_rows)
        m_prev = m_sc[rs, :]
        m_new = jnp.maximum(m_prev, jnp.max(s, axis=1, keepdims=True))
        alpha = jnp.exp(m_prev - m_new)
        p = jnp.exp(s - _lanes(m_new, keys))
        l_sc[rs, :] = alpha * l_sc[rs, :] + jnp.sum(p, axis=1, keepdims=True)
        acc_sc[rs, :] = _lanes(alpha, dv) * acc_sc[rs, :]
        m_sc[rs, :] = m_new
        p_sc[rs, :] = p.astype(BF16)
        return carry

    lax.fori_loop(0, rows // sub, step, 0)


def _add_group_bias(s, bias, n_groups):
    g, keys = bias.shape
    return (s.reshape(n_groups, g, keys) + bias[None]).reshape(n_groups * g, keys)


def _mla_prompt_kernel(ql_ref, qr_ref, kv_ref, o_ref, m_sc, l_sc, acc_sc, s_sc, p_sc, *, tk, sub):
    qi = pl.program_id(1)
    kb = pl.program_id(2)
    rows = MLA_HEADS * Q_BLOCK
    last = ((qi + 1) * Q_BLOCK - 1) // tk
    visible = (kb + 1) * tk <= qi * Q_BLOCK

    def tile(bias):
        ql = ql_ref[...].reshape(rows, MLA_KV_LORA)
        qr = qr_ref[...].reshape(rows, MLA_ROPE)
        kv = kv_ref[...]
        c = kv[:, :MLA_KV_LORA].astype(BF16)
        kr = kv[:, MLA_KV_LORA:].astype(BF16)
        s_sc[...] = _dot_nt(ql, c) + _dot_nt(qr, kr)
        _flash_rows(s_sc, p_sc, bias, m_sc, l_sc, acc_sc, group_rows=Q_BLOCK, sub=sub)
        acc_sc[...] += _dot(p_sc[...], c)

    @pl.when(kb == 0)
    def _():
        _softmax_init(m_sc, l_sc, acc_sc)

    @pl.when(visible)
    def _():
        tile(None)

    @pl.when(jnp.logical_and(kb <= last, jnp.logical_not(visible)))
    def _():
        qpos = qi * Q_BLOCK + lax.broadcasted_iota(jnp.int32, (Q_BLOCK, tk), 0)
        kpos = kb * tk + lax.broadcasted_iota(jnp.int32, (Q_BLOCK, tk), 1)
        tile(jnp.where(kpos <= qpos, 0.0, NEG_INF))

    @pl.when(kb == pl.num_programs(2) - 1)
    def _():
        o = acc_sc[...] / l_sc[:, :1]
        o_ref[...] = o.reshape(MLA_HEADS, Q_BLOCK, MLA_KV_LORA).astype(o_ref.dtype)


def _mla_prompt(ql, qr, kv, *, tk=512, sub=512):
    nq = SEQ // Q_BLOCK
    nk = SEQ // tk
    rows = MLA_HEADS * Q_BLOCK

    def kv_map(b, qi, kb):
        return (b, jnp.minimum(kb, ((qi + 1) * Q_BLOCK - 1) // tk), 0)

    est = (2 * (rows * MLA_KV_LORA * 2 + rows * LANE * 2 + tk * 3 * LANE * 4 + rows * MLA_KV_LORA * 2)
           + rows * (MLA_KV_LORA + 2 * LANE) * 4 + 3 * rows * tk * 4 + rows * MLA_KV_LORA * 4)
    return pl.pallas_call(
        functools.partial(_mla_prompt_kernel, tk=tk, sub=sub),
        out_shape=jax.ShapeDtypeStruct((MLA_HEADS, BATCH, SEQ, MLA_KV_LORA), BF16),
        grid=(BATCH, nq, nk),
        in_specs=[pl.BlockSpec((MLA_HEADS, None, Q_BLOCK, MLA_KV_LORA), lambda b, qi, kb: (0, b, qi, 0)),
                  pl.BlockSpec((MLA_HEADS, None, Q_BLOCK, MLA_ROPE), lambda b, qi, kb: (0, b, qi, 0)),
                  pl.BlockSpec((None, tk, MLA_CACHE), kv_map)],
        out_specs=pl.BlockSpec((MLA_HEADS, None, Q_BLOCK, MLA_KV_LORA), lambda b, qi, kb: (0, b, qi, 0)),
        scratch_shapes=[pltpu.VMEM((rows, LANE), F32), pltpu.VMEM((rows, LANE), F32),
                        pltpu.VMEM((rows, MLA_KV_LORA), F32), pltpu.VMEM((rows, tk), F32),
                        pltpu.VMEM((rows, tk), BF16)],
        compiler_params=_cparams(("arbitrary", "arbitrary", "arbitrary"), est),
        name="mla_prompt",
    )(ql, qr, kv)


def _mla_sample_kernel(pt_ref, ql_ref, qr_ref, new_ref, *rest, pp):
    del pt_ref
    pages = rest[:pp]
    o_ref, m_sc, l_sc, acc_sc = rest[pp:]
    ch = pl.program_id(1)
    rows = DEC_SEQ * MLA_HEADS

    @pl.when(ch == 0)
    def _():
        _softmax_init(m_sc, l_sc, acc_sc)

    ql = ql_ref[...]
    qr = qr_ref[...]
    kvt = jnp.concatenate([p[...] for p in pages], axis=1)
    ct = kvt[:MLA_KV_LORA].astype(BF16)
    krt = kvt[MLA_KV_LORA:].astype(BF16)
    s = _dot(ql, ct) + _dot(qr, krt)
    _softmax_update(s, ct, m_sc, l_sc, acc_sc, v_transposed=True)

    @pl.when(ch == pl.num_programs(1) - 1)
    def _():
        kvn = new_ref[...]
        cn = kvn[:MLA_KV_LORA].astype(BF16)
        krn = kvn[MLA_KV_LORA:].astype(BF16)
        sn = _dot(ql, cn) + _dot(qr, krn)
        t = lax.broadcasted_iota(jnp.int32, (rows, PAGE_SIZE), 0) >> HEAD_SHIFT
        j = lax.broadcasted_iota(jnp.int32, (rows, PAGE_SIZE), 1)
        sn = jnp.where(j <= t, sn, NEG_INF)
        _softmax_update(sn, cn, m_sc, l_sc, acc_sc, v_transposed=True)
        o_ref[...] = (acc_sc[...] / l_sc[...]).astype(o_ref.dtype)


def _mla_sample(page_table, ql, qr, new_rows, cache, layer, *, pp=32):
    rows = DEC_SEQ * MLA_HEADS
    nch = N_PAGES // pp

    def page_spec(p):
        return pl.BlockSpec((None, None, MLA_CACHE, PAGE_SIZE),
                            lambda b, ch, pt: (pt[b, ch * pp + p], layer, 0, 0))

    est = (2 * (pp + 1) * PAGE_SIZE * 3 * LANE * 4 + 3 * pp * PAGE_SIZE * 3 * LANE * 4
           + 6 * rows * pp * PAGE_SIZE * 4)
    grid_spec = pltpu.PrefetchScalarGridSpec(
        num_scalar_prefetch=1,
        grid=(DEC_BATCH, nch),
        in_specs=[pl.BlockSpec((None, rows, MLA_KV_LORA), lambda b, ch, pt: (b, 0, 0)),
                  pl.BlockSpec((None, rows, MLA_ROPE), lambda b, ch, pt: (b, 0, 0)),
                  pl.BlockSpec((None, MLA_CACHE, PAGE_SIZE), lambda b, ch, pt: (b, 0, 0))]
        + [page_spec(p) for p in range(pp)],
        out_specs=pl.BlockSpec((None, rows, MLA_KV_LORA), lambda b, ch, pt: (b, 0, 0)),
        scratch_shapes=[pltpu.VMEM((rows, 1), F32), pltpu.VMEM((rows, 1), F32),
                        pltpu.VMEM((rows, MLA_KV_LORA), F32)],
    )
    return pl.pallas_call(
        functools.partial(_mla_sample_kernel, pp=pp),
        out_shape=jax.ShapeDtypeStruct((DEC_BATCH, rows, MLA_KV_LORA), BF16),
        grid_spec=grid_spec,
        compiler_params=_cparams(("arbitrary", "arbitrary"), est),
        name="mla_sample",
    )(page_table, ql, qr, new_rows, *([cache] * pp))


def _conv_prompt_kernel(prev_ref, cur_ref, w_ref, b_ref, g_ref, beta_ref, o_ref, ext_sc, *, tm):
    i = pl.program_id(1)
    halo = 32
    sub = 32
    prev = prev_ref[...]
    ext_sc[0:halo, :] = jnp.where(i > 0, prev, 0.0)
    ext_sc[halo:halo + tm, :] = cur_ref[...]
    for r0 in range(0, tm, sub):
        acc = jnp.zeros((sub, CONV_DIM), F32)
        for k in range(CONV_WIDTH):
            acc = acc + ext_sc[pl.ds(r0 + halo - (CONV_WIDTH - 1) + k, sub), :] * w_ref[k:k + 1, :]
        y = acc + b_ref[...]
        mu = jnp.mean(y, axis=1, keepdims=True)
        yc = y - mu
        var = jnp.mean(yc * yc, axis=1, keepdims=True)
        z = yc * lax.rsqrt(var + LN_EPS) * g_ref[...] + beta_ref[...]
        o_ref[r0:r0 + sub, :] = (z * jax.nn.sigmoid(z)).astype(o_ref.dtype)


def _conv_prompt(u, w, b, g, beta, *, tm=256):
    halo = 32
    est = 2 * (halo + 2 * tm) * CONV_DIM * 4 + (halo + 6 * tm) * CONV_DIM * 4
    vec = pl.BlockSpec((1, CONV_DIM), lambda bb, i: (0, 0))
    return pl.pallas_call(
        functools.partial(_conv_prompt_kernel, tm=tm),
        out_shape=jax.ShapeDtypeStruct((BATCH, SEQ, CONV_DIM), BF16),
        grid=(BATCH, SEQ // tm),
        in_specs=[pl.BlockSpec((None, halo, CONV_DIM),
                               lambda bb, i: (bb, jnp.maximum(i * (tm // halo) - 1, 0), 0)),
                  pl.BlockSpec((None, tm, CONV_DIM), lambda bb, i: (bb, i, 0)),
                  pl.BlockSpec((CONV_WIDTH, CONV_DIM), lambda bb, i: (0, 0)),
                  vec, vec, vec],
        out_specs=pl.BlockSpec((None, tm, CONV_DIM), lambda bb, i: (bb, i, 0)),
        scratch_shapes=[pltpu.VMEM((halo + tm, CONV_DIM), F32)],
        compiler_params=_cparams(("arbitrary", "arbitrary"), est),
        name="conv_prompt",
    )(u, u, w, b.reshape(1, -1), g.reshape(1, -1), beta.reshape(1, -1))


def _masked_softmax_rows(s, mask):
    s = jnp.where(mask, s, NEG_INF)
    m = jnp.max(s, axis=1, keepdims=True)
    p = jnp.where(mask, jnp.exp(s - m), 0.0)
    l = jnp.sum(p, axis=1, keepdims=True)
    return p / jnp.where(l > 0.0, l, 1.0)


def _select_top_blocks(imp, qpos):
    rows, nbs = imp.shape
    blk = lax.broadcasted_iota(jnp.int32, (rows, nbs), 1)
    cur = qpos >> BLK_SHIFT
    forced = (blk == 0) | (blk == cur) | (blk == cur - 1)
    score = jnp.where(blk > cur, -1.0, jnp.where(forced, FORCE_SCORE, imp))
    sel = jnp.zeros((rows, nbs), F32)
    for _ in range(min(NSA_TOPN, nbs)):
        m = jnp.max(score, axis=1, keepdims=True)
        first = jnp.min(jnp.where(score == m, blk, nbs), axis=1, keepdims=True)
        hit = blk == first
        sel = jnp.where(hit & (m >= 0.0), 1.0, sel)
        score = jnp.where(hit, -2.0, score)
    return sel


def _nsa_cmp_prompt_kernel(q_ref, kcvc_ref, oc_ref, sel_ref, *, tq):
    i = pl.program_id(1)
    nb = kcvc_ref.shape[0]
    kcvc = kcvc_ref[...]
    kc = kcvc[:, :NSA_DH].astype(BF16)
    vc = kcvc[:, NSA_DH:].astype(BF16)
    qpos = i * tq + lax.broadcasted_iota(jnp.int32, (tq, nb), 0)
    bend = (lax.broadcasted_iota(jnp.int32, (tq, nb), 1) + 1) * NSA_BLK - 1
    mask = bend <= qpos
    q = q_ref[...].astype(BF16)
    imp = jnp.zeros((tq, nb), F32)
    outs = []
    for h in range(NSA_HEADS):
        s = _dot_nt(q[:, h * NSA_DH:(h + 1) * NSA_DH], kc) * NSA_SCALE
        p = _masked_softmax_rows(s, mask)
        imp = imp + p
        outs.append(_dot(p.astype(BF16), vc))
    oc_ref[...] = jnp.concatenate(outs, axis=1)
    sel_ref[...] = _select_top_blocks(imp, qpos)


def _nsa_cmp_prompt(qn, kcvc, *, tq=256):
    nb = kcvc.shape[1]
    hd = NSA_HEADS * NSA_DH
    est = 2 * (tq * hd * 4 * 2 + nb * LANE * 4 + tq * LANE * 4) + 8 * tq * hd * 4
    return pl.pallas_call(
        functools.partial(_nsa_cmp_prompt_kernel, tq=tq),
        out_shape=(jax.ShapeDtypeStruct((BATCH, SEQ, hd), F32),
                   jax.ShapeDtypeStruct((BATCH, SEQ, nb), F32)),
        grid=(BATCH, SEQ // tq),
        in_specs=[pl.BlockSpec((None, tq, hd), lambda b, i: (b, i, 0)),
                  pl.BlockSpec((None, nb, 2 * NSA_DH), lambda b, i: (b, 0, 0))],
        out_specs=(pl.BlockSpec((None, tq, hd), lambda b, i: (b, i, 0)),
                   pl.BlockSpec((None, tq, nb), lambda b, i: (b, i, 0))),
        compiler_params=_cparams(("arbitrary", "arbitrary"), est),
        name="nsa_cmp_prompt",
    )(qn, kcvc)


def _nsa_cmp_sample_kernel(pt_ref, q_ref, pe_ref, w1_ref, w2_ref, *rest):
    del pt_ref
    pages = rest[:N_PAGES]
    oc_ref, imp_ref, x_sc = rest[N_PAGES:]
    nb = 2 * N_PAGES
    for p in range(N_PAGES):
        pg = pages[p][...]
        x_sc[pl.ds((2 * p) * CMP_PITCH, NSA_BLK), :] = pg[:NSA_BLK]
        x_sc[pl.ds((2 * p + 1) * CMP_PITCH, NSA_BLK), :] = pg[NSA_BLK:]
    acc = jnp.zeros((nb, 2 * NSA_PHI_HID), F32)
    for r2 in range(NSA_BLK // 2):
        a0 = x_sc[pl.ds(2 * r2, nb, stride=CMP_PITCH), :] + pe_ref[2 * r2:2 * r2 + 1, :]
        a1 = x_sc[pl.ds(2 * r2 + 1, nb, stride=CMP_PITCH), :] + pe_ref[2 * r2 + 1:2 * r2 + 2, :]
        a = jnp.concatenate([a0, a1], axis=1).astype(BF16)
        acc = acc + _dot(a, w1_ref[r2])
    hid = jax.nn.gelu(acc)
    kcvc = _dot(hid.astype(BF16), w2_ref[...])
    kc = kcvc[:, :NSA_DH].astype(BF16)
    vc = kcvc[:, NSA_DH:].astype(BF16)
    rows = DEC_SEQ * NSA_HEADS
    s = _dot_nt(q_ref[...], kc) * NSA_SCALE
    qpos = PAST_LEN + (lax.broadcasted_iota(jnp.int32, (rows, nb), 0) >> HEAD_SHIFT)
    bend = (lax.broadcasted_iota(jnp.int32, (rows, nb), 1) + 1) * NSA_BLK - 1
    p = _masked_softmax_rows(s, bend <= qpos)
    oc_ref[...] = _dot(p.astype(BF16), vc)
    imp_ref[...] = jnp.sum(p.reshape(DEC_SEQ, NSA_HEADS, nb), axis=1)


def _nsa_cmp_sample(page_table, qn, pe2, w1p, w2bd, cache, layer):
    rows = DEC_SEQ * NSA_HEADS
    nb = 2 * N_PAGES

    def page_spec(p):
        return pl.BlockSpec((None, None, PAGE_SIZE, 2 * NSA_DH), lambda b, pt: (pt[b, p], layer, 0, 0))

    est = (2 * N_PAGES * PAGE_SIZE * LANE * 4 + nb * CMP_PITCH * LANE * 4
           + 2 * (NSA_BLK // 2) * 2 * LANE * LANE * 2 + (8 << 20))
    grid_spec = pltpu.PrefetchScalarGridSpec(
        num_scalar_prefetch=1,
        grid=(DEC_BATCH,),
        in_specs=[pl.BlockSpec((None, rows, NSA_DH), lambda b, pt: (b, 0, 0)),
                  pl.BlockSpec((NSA_BLK, 2 * NSA_DH), lambda b, pt: (0, 0)),
                  pl.BlockSpec((NSA_BLK // 2, 4 * NSA_DH, 2 * NSA_PHI_HID), lambda b, pt: (0, 0, 0)),
                  pl.BlockSpec((2 * NSA_PHI_HID, 2 * NSA_DH), lambda b, pt: (0, 0))]
        + [page_spec(p) for p in range(N_PAGES)],
        out_specs=(pl.BlockSpec((None, rows, NSA_DH), lambda b, pt: (b, 0, 0)),
                   pl.BlockSpec((None, DEC_SEQ, nb), lambda b, pt: (b, 0, 0))),
        scratch_shapes=[pltpu.VMEM((nb * CMP_PITCH, 2 * NSA_DH), F32)],
    )
    return pl.pallas_call(
        _nsa_cmp_sample_kernel,
        out_shape=(jax.ShapeDtypeStruct((DEC_BATCH, rows, NSA_DH), F32),
                   jax.ShapeDtypeStruct((DEC_BATCH, DEC_SEQ, nb), F32)),
        grid_spec=grid_spec,
        compiler_params=_cparams(("arbitrary",), est),
        name="nsa_cmp_sample",
    )(page_table, qn, pe2, w1p, w2bd, *([cache] * N_PAGES))


def _split_heads(q):
    return jnp.concatenate([q[:, h * NSA_DH:(h + 1) * NSA_DH] for h in range(NSA_HEADS)], axis=0).astype(BF16)


def _merge_heads(o):
    return jnp.concatenate([o[h * Q_BLOCK:(h + 1) * Q_BLOCK] for h in range(NSA_HEADS)], axis=1)


def _nsa_sel_prompt_kernel(q_ref, kv_ref, sel_ref, o_ref, m_sc, l_sc, acc_sc, s_sc, p_sc, *, tk, sub):
    qi = pl.program_id(1)
    kb = pl.program_id(2)
    rows = NSA_HEADS * Q_BLOCK
    last = ((qi + 1) * Q_BLOCK - 1) // tk
    nbs = sel_ref.shape[1]

    @pl.when(kb == 0)
    def _():
        _softmax_init(m_sc, l_sc, acc_sc)

    @pl.when(kb <= last)
    def _():
        q = _split_heads(q_ref[...])
        kv = kv_ref[...]
        k = kv[:, :NSA_DH].astype(BF16)
        v = kv[:, NSA_DH:].astype(BF16)
        kpos_e = kb * tk + lax.broadcasted_iota(jnp.int32, (nbs, tk), 1)
        expand = ((kpos_e >> BLK_SHIFT) == lax.broadcasted_iota(jnp.int32, (nbs, tk), 0)).astype(BF16)
        chosen = _dot(sel_ref[...].astype(BF16), expand)
        qpos = qi * Q_BLOCK + lax.broadcasted_iota(jnp.int32, (Q_BLOCK, tk), 0)
        kpos = kb * tk + lax.broadcasted_iota(jnp.int32, (Q_BLOCK, tk), 1)
        bias = jnp.where((chosen > 0.5) & (kpos <= qpos), 0.0, NEG_INF)
        s_sc[...] = _dot_nt(q, k)
        _flash_rows(s_sc, p_sc, bias, m_sc, l_sc, acc_sc, group_rows=Q_BLOCK, sub=sub)
        acc_sc[...] += _dot(p_sc[...], v)

    @pl.when(kb == pl.num_programs(2) - 1)
    def _():
        o = acc_sc[...] / l_sc[:, :1]
        o_ref[...] = _merge_heads(o)


def _nsa_sel_prompt(qr, kv_sel, selmask, *, tk=512, sub=512):
    nq = SEQ // Q_BLOCK
    nk = SEQ // tk
    rows = NSA_HEADS * Q_BLOCK
    nbs = selmask.shape[2]

    def kv_map(b, qi, kb):
        return (b, jnp.minimum(kb, ((qi + 1) * Q_BLOCK - 1) // tk), 0)

    est = 2 * (rows * LANE * 2 + tk * LANE * 4 + Q_BLOCK * LANE * 4 + rows * LANE * 4) + 3 * rows * LANE * 4 \
        + 5 * rows * tk * 4
    return pl.pallas_call(
        functools.partial(_nsa_sel_prompt_kernel, tk=tk, sub=sub),
        out_shape=jax.ShapeDtypeStruct((BATCH, SEQ, NSA_HEADS * NSA_DH), F32),
        grid=(BATCH, nq, nk),
        in_specs=[pl.BlockSpec((None, Q_BLOCK, NSA_HEADS * NSA_DH), lambda b, qi, kb: (b, qi, 0)),
                  pl.BlockSpec((None, tk, 2 * NSA_DH), kv_map),
                  pl.BlockSpec((None, Q_BLOCK, nbs), lambda b, qi, kb: (b, qi, 0))],
        out_specs=pl.BlockSpec((None, Q_BLOCK, NSA_HEADS * NSA_DH), lambda b, qi, kb: (b, qi, 0)),
        scratch_shapes=[pltpu.VMEM((rows, LANE), F32), pltpu.VMEM((rows, LANE), F32),
                        pltpu.VMEM((rows, NSA_DH), F32), pltpu.VMEM((rows, tk), F32),
                        pltpu.VMEM((rows, tk), BF16)],
        compiler_params=_cparams(("arbitrary", "arbitrary", "arbitrary"), est),
        name="nsa_sel_prompt",
    )(qr, kv_sel, selmask)


def _nsa_win_prompt_kernel(q_ref, *rest, nw):
    tiles = rest[:nw]
    o_ref = rest[nw]
    qi = pl.program_id(1)
    nk = nw * Q_BLOCK
    q = _split_heads(q_ref[...])
    kv = jnp.concatenate([t[...] for t in tiles], axis=0)
    k = kv[:, :NSA_DH].astype(BF16)
    v = kv[:, NSA_DH:].astype(BF16)
    qpos = qi * Q_BLOCK + lax.broadcasted_iota(jnp.int32, (Q_BLOCK, nk), 0)
    kpos = (qi - (nw - 1)) * Q_BLOCK + lax.broadcasted_iota(jnp.int32, (Q_BLOCK, nk), 1)
    bias = jnp.where((kpos <= qpos) & (kpos > qpos - NSA_WINDOW) & (kpos >= 0), 0.0, NEG_INF)
    s = _add_group_bias(_dot_nt(q, k), bias, NSA_HEADS)
    p = jnp.exp(s - jnp.max(s, axis=1, keepdims=True))
    o = _dot(p.astype(BF16), v) / jnp.sum(p, axis=1, keepdims=True)
    o_ref[...] = _merge_heads(o)


def _nsa_win_prompt(qr, kv_win):
    nq = SEQ // Q_BLOCK
    nw = NSA_WINDOW // Q_BLOCK + 1
    rows = NSA_HEADS * Q_BLOCK
    nk = nw * Q_BLOCK

    def tile_spec(w):
        return pl.BlockSpec((None, Q_BLOCK, 2 * NSA_DH),
                            lambda b, qi: (b, jnp.maximum(qi - (nw - 1) + w, 0), 0))

    est = 2 * (rows * LANE * 2 + nk * LANE * 4 + rows * LANE * 4) + 3 * rows * LANE * 4 + 3 * rows * nk * 4
    return pl.pallas_call(
        functools.partial(_nsa_win_prompt_kernel, nw=nw),
        out_shape=jax.ShapeDtypeStruct((BATCH, SEQ, NSA_HEADS * NSA_DH), F32),
        grid=(BATCH, nq),
        in_specs=[pl.BlockSpec((None, Q_BLOCK, NSA_HEADS * NSA_DH), lambda b, qi: (b, qi, 0))]
        + [tile_spec(w) for w in range(nw)],
        out_specs=pl.BlockSpec((None, Q_BLOCK, NSA_HEADS * NSA_DH), lambda b, qi: (b, qi, 0)),
        compiler_params=_cparams(("arbitrary", "arbitrary"), est),
        name="nsa_win_prompt",
    )(qr, *([kv_win] * nw))


def _nsa_selwin_sample_kernel(pt_ref, q_ref, allowed_ref, selnew_ref, win_ref, winnew_ref, *rest):
    del pt_ref
    pages = rest[:N_PAGES]
    os_ref, ow_ref = rest[N_PAGES:]
    rows = DEC_SEQ * NSA_HEADS
    q = q_ref[...]
    kv = jnp.concatenate([p[...] for p in pages] + [selnew_ref[...]], axis=0)
    nk = kv.shape[0]
    k = kv[:, :NSA_DH].astype(BF16)
    v = kv[:, NSA_DH:].astype(BF16)
    allowed = jnp.broadcast_to(allowed_ref[...][:, None, :], (DEC_SEQ, NSA_HEADS, nk)).reshape(rows, nk)
    p = _masked_softmax_rows(_dot_nt(q, k), allowed > 0.5)
    os_ref[...] = _dot(p.astype(BF16), v)
    kvw = jnp.concatenate([win_ref[...], winnew_ref[...]], axis=0)
    nkw = kvw.shape[0]
    wb = win_ref.shape[0]
    kw = kvw[:, :NSA_DH].astype(BF16)
    vw = kvw[:, NSA_DH:].astype(BF16)
    t = lax.broadcasted_iota(jnp.int32, (rows, nkw), 0) >> HEAD_SHIFT
    j = lax.broadcasted_iota(jnp.int32, (rows, nkw), 1)
    kp = PAST_LEN - wb + j
    qp = PAST_LEN + t
    wmask = (kp <= qp) & (kp > qp - NSA_WINDOW) & (kp >= 0) & (j < wb + DEC_SEQ)
    pw = _masked_softmax_rows(_dot_nt(q, kw), wmask)
    ow_ref[...] = _dot(pw.astype(BF16), vw)


def _nsa_selwin_sample(page_table, qr, allowed, sel_new, win_state, win_new, cache, layer):
    rows = DEC_SEQ * NSA_HEADS
    nk = PAST_LEN + PAGE_SIZE
    wb = win_state.shape[2]

    def page_spec(p):
        return pl.BlockSpec((None, None, PAGE_SIZE, 2 * NSA_DH), lambda b, pt: (pt[b, p], layer, 0, 0))

    est = 2 * (N_PAGES + 2) * PAGE_SIZE * LANE * 4 + 2 * wb * LANE * 4 + 3 * nk * LANE * 4 + 8 * rows * nk * 4
    grid_spec = pltpu.PrefetchScalarGridSpec(
        num_scalar_prefetch=1,
        grid=(DEC_BATCH,),
        in_specs=[pl.BlockSpec((None, rows, NSA_DH), lambda b, pt: (b, 0, 0)),
                  pl.BlockSpec((None, DEC_SEQ, nk), lambda b, pt: (b, 0, 0)),
                  pl.BlockSpec((None, PAGE_SIZE, 2 * NSA_DH), lambda b, pt: (b, 0, 0)),
                  pl.BlockSpec((None, None, wb, 2 * NSA_DH), lambda b, pt: (b, layer, 0, 0)),
                  pl.BlockSpec((None, PAGE_SIZE, 2 * NSA_DH), lambda b, pt: (b, 0, 0))]
        + [page_spec(p) for p in range(N_PAGES)],
        out_specs=(pl.BlockSpec((None, rows, NSA_DH), lambda b, pt: (b, 0, 0)),
                   pl.BlockSpec((None, rows, NSA_DH), lambda b, pt: (b, 0, 0))),
    )
    return pl.pallas_call(
        _nsa_selwin_sample_kernel,
        out_shape=(jax.ShapeDtypeStruct((DEC_BATCH, rows, NSA_DH), F32),
                   jax.ShapeDtypeStruct((DEC_BATCH, rows, NSA_DH), F32)),
        grid_spec=grid_spec,
        compiler_params=_cparams(("arbitrary",), est),
        name="nsa_selwin_sample",
    )(page_table, qr, allowed, sel_new, win_state, win_new, *([cache] * N_PAGES))


def _moe_row_copy(x_hbm, xbuf, sem, tok, slot, r):
    return pltpu.make_async_copy(x_hbm.at[pl.ds(tok, 1)], xbuf.at[slot, pl.ds(r, 1)], sem.at[slot])


def _moe_kernel(be_ref, nu_ref, tok0_ref, tokn_ref, x_hbm, sw_ref, wgu_ref, bgu_ref, wdn_ref, bdn_ref,
                o_ref, xbuf, sem):
    del be_ref
    i = pl.program_id(0)
    n_used = nu_ref[0]

    def start_rows(tok_ref, slot):
        def body(r, carry):
            _moe_row_copy(x_hbm, xbuf, sem, tok_ref[0, r], slot, r).start()
            return carry
        lax.fori_loop(0, MOE_TM, body, 0, unroll=8)

    def wait_rows(slot):
        def body(r, carry):
            _moe_row_copy(x_hbm, xbuf, sem, 0, slot, r).wait()
            return carry
        lax.fori_loop(0, MOE_TM, body, 0, unroll=8)

    @pl.when(i == 0)
    def _():
        start_rows(tok0_ref, 0)

    @pl.when(i + 1 < n_used)
    def _():
        start_rows(tokn_ref, (i + 1) % 2)

    @pl.when(i < n_used)
    def _():
        slot = i % 2
        wait_rows(slot)
        h = _dot(xbuf[slot].astype(BF16), wgu_ref[...]) + bgu_ref[...]
        gate = jnp.minimum(h[:, :D_FF], SWIGLU_LIMIT)
        up = jnp.clip(h[:, D_FF:], -SWIGLU_LIMIT, SWIGLU_LIMIT)
        act = gate * jax.nn.sigmoid(SWIGLU_ALPHA * gate) * (up + 1.0)
        y = _dot(act.astype(BF16), wdn_ref[...]) + bdn_ref[...]
        o_ref[...] = y * sw_ref[...]

    @pl.when(i >= n_used)
    def _():
        o_ref[...] = jnp.zeros(o_ref.shape, F32)


def _moe_experts(blk_e, n_used, slot_tok, x, slot_w, w_gu, b_gu, w_dn, b_dn):
    tm = MOE_TM

    def row_map(i, be, nu):
        return (jnp.minimum(i, nu[0] - 1), 0)

    def exp_map(i, be, nu):
        return (be[jnp.minimum(i, nu[0] - 1)], 0, 0)

    est = 2 * (tm * LANE * 4 + D_MODEL * 2 * D_FF * 2 + D_FF * D_MODEL * 2 + tm * D_MODEL * 4) \
        + 2 * tm * D_MODEL * 4 + tm * D_MODEL * 2 + 6 * tm * 2 * D_FF * 4
    grid_spec = pltpu.PrefetchScalarGridSpec(
        num_scalar_prefetch=2,
        grid=(MOE_BLOCKS,),
        in_specs=[pl.BlockSpec((None, 1, tm), lambda i, be, nu: (0, 0, 0), memory_space=pltpu.SMEM),
                  pl.BlockSpec((None, 1, tm), lambda i, be, nu: (jnp.minimum(i + 1, MOE_BLOCKS - 1), 0, 0),
                               memory_space=pltpu.SMEM),
                  pl.BlockSpec(memory_space=pl.ANY),
                  pl.BlockSpec((tm, 1), row_map),
                  pl.BlockSpec((None, D_MODEL, 2 * D_FF), exp_map),
                  pl.BlockSpec((None, 1, 2 * D_FF), exp_map),
                  pl.BlockSpec((None, D_FF, D_MODEL), exp_map),
                  pl.BlockSpec((None, 1, D_MODEL), exp_map)],
        out_specs=pl.BlockSpec((tm, D_MODEL), lambda i, be, nu: (i, 0)),
        scratch_shapes=[pltpu.VMEM((2, tm, D_MODEL), F32), pltpu.SemaphoreType.DMA((2,))],
    )
    return pl.pallas_call(
        _moe_kernel,
        out_shape=jax.ShapeDtypeStruct((MOE_SLOTS, D_MODEL), F32),
        grid_spec=grid_spec,
        compiler_params=_cparams(("arbitrary",), est),
        name="moe_experts",
    )(blk_e, n_used, slot_tok, slot_tok, x, slot_w, w_gu, b_gu, w_dn, b_dn)


def _rms_norm(x, g):
    return x * lax.rsqrt(jnp.mean(x * x, axis=-1, keepdims=True) + RMS_EPS) * g


def _layer_norm(x, g, b):
    xc = x - jnp.mean(x, axis=-1, keepdims=True)
    var = jnp.mean(xc * xc, axis=-1, keepdims=True)
    return xc * lax.rsqrt(var + LN_EPS) * g + b


def _rope(x, pos, rot_dim):
    half = rot_dim // 2
    freqs = ROPE_THETA ** (-jnp.arange(half, dtype=F32) * (2.0 / rot_dim))
    ang = pos.astype(F32)[:, None] * freqs[None, :]
    shape = (ang.shape[0],) + (1,) * (x.ndim - 2) + (half,)
    cos = jnp.cos(ang).reshape(shape)
    sin = jnp.sin(ang).reshape(shape)
    x1, x2, rest = x[..., :half], x[..., half:rot_dim], x[..., rot_dim:]
    return jnp.concatenate([x1 * cos - x2 * sin, x2 * cos + x1 * sin, rest], axis=-1)


def _pad_rows(x, n):
    return jnp.pad(x, ((0, 0), (0, n - x.shape[1]), (0, 0)))


def _select_blocks(imp, q_pos, nbs):
    nb = imp.shape[-1]
    imp = jnp.pad(imp, ((0, 0), (0, 0), (0, nbs - nb)), constant_values=-1.0)
    blk = jnp.arange(nbs)[None, :]
    cur = (q_pos // NSA_BLK)[:, None]
    forced = (blk == 0) | (blk == cur) | (blk == cur - 1)
    score = jnp.where(blk > cur, -1.0, jnp.where(forced, FORCE_SCORE, imp))
    n_sel = min(NSA_TOPN, nbs)
    top_s, top_i = lax.top_k(score, n_sel)
    onehot = (top_i[..., None] == jnp.arange(nbs)) & (top_s >= 0.0)[..., None]
    return jnp.any(onehot, axis=-2).astype(F32)


def _moe(x, router_w, router_b, w_gu, b_gu, w_dn, b_dn, layer):
    n_tok = x.shape[0]
    n_asg = n_tok * TOP_K
    logits = _mm_f32(x, router_w, tm=512, name=f"router_l{layer}") + router_b
    top_v, top_e = lax.top_k(logits, TOP_K)
    gates = jax.nn.softmax(top_v, axis=-1)
    flat_e = top_e.reshape(-1)
    onehot = (flat_e[:, None] == jnp.arange(N_EXPERTS)[None, :]).astype(jnp.int32)
    running = jnp.cumsum(onehot, axis=0)
    rank = jnp.sum((running - 1) * onehot, axis=1)
    counts = running[-1]
    padded = (counts + MOE_TM - 1) // MOE_TM * MOE_TM
    pad_end = jnp.cumsum(padded)
    pad_start = pad_end - padded
    dest = (pad_start[flat_e] + rank).astype(jnp.int32)
    tok_of = (jnp.arange(n_asg) // TOP_K).astype(jnp.int32)
    slot_tok = jnp.zeros((MOE_SLOTS,), jnp.int32).at[dest].set(tok_of)
    slot_w = jnp.zeros((MOE_SLOTS,), F32).at[dest].set(gates.reshape(-1))
    blk_e = jnp.minimum(jnp.searchsorted(pad_end, jnp.arange(MOE_BLOCKS) * MOE_TM, side='right'),
                        N_EXPERTS - 1).astype(jnp.int32)
    n_used = (pad_end[-1] // MOE_TM).astype(jnp.int32).reshape(1)
    ys = _moe_experts(blk_e, n_used, slot_tok.reshape(MOE_BLOCKS, 1, MOE_TM), x, slot_w[:, None],
                      w_gu, b_gu[:, None, :], w_dn, b_dn[:, None, :])
    return jnp.sum(ys[dest.reshape(n_tok, TOP_K)], axis=1)


def _prep_weights(w_in, mla_w_uq, mla_w_uk, mla_w_uv, nsa_phi_w1, nsa_phi_w2):
    d = w_in.shape[0]
    z = lambda n: jnp.zeros((d, D_MODEL, n), F32)
    w_in_p = jnp.concatenate([w_in[:, :, :672], z(96), w_in[:, :, 672:4176], z(80), w_in[:, :, 4176:],
                              z(IN_PAD_N - IN_GM - 3 * D_MODEL)], axis=2).astype(BF16)
    uq = mla_w_uq.reshape(d, MLA_Q_LORA, MLA_HEADS, MLA_NOPE + MLA_ROPE)
    w_uq_p = jnp.concatenate([uq[..., :MLA_NOPE].reshape(d, MLA_Q_LORA, -1),
                              uq[..., MLA_NOPE:].reshape(d, MLA_Q_LORA, -1)], axis=2).astype(BF16)
    w_uk_t = jnp.transpose(mla_w_uk, (0, 2, 3, 1)).astype(BF16)
    w_uv_t = jnp.transpose(mla_w_uv, (0, 2, 1, 3)).astype(BF16)
    w1 = nsa_phi_w1.reshape(d, 2, NSA_BLK, NSA_DH, NSA_PHI_HID)
    zero = jnp.zeros_like(w1[:, 0])
    w1k = jnp.concatenate([w1[:, 0], zero], axis=-1)
    w1v = jnp.concatenate([zero, w1[:, 1]], axis=-1)
    w1r = jnp.concatenate([w1k, w1v], axis=2)
    w1_flat = w1r.reshape(d, NSA_BLK * 2 * NSA_DH, 2 * NSA_PHI_HID).astype(BF16)
    w1_pair = w1r.reshape(d, NSA_BLK // 2, 4 * NSA_DH, 2 * NSA_PHI_HID).astype(BF16)
    z2 = jnp.zeros((d, NSA_PHI_HID, NSA_DH), F32)
    w2bd = jnp.concatenate([jnp.concatenate([nsa_phi_w2[:, 0], z2], axis=2),
                            jnp.concatenate([z2, nsa_phi_w2[:, 1]], axis=2)], axis=1).astype(BF16)
    return w_in_p, w_uq_p, w_uk_t, w_uv_t, w1_flat, w1_pair, w2bd


def kernel(x_prompt, x_sample, cache_mla, cache_nsa_cmp, cache_nsa_sel, state_nsa_win, state_conv, page_table,
           w_in, mla_q_norm, mla_kv_norm, mla_w_uq, mla_w_uk, mla_w_uv, mla_w_br,
           conv_w, conv_b, conv_ln_g, conv_ln_b, conv_w_br,
           nsa_phi_pe, nsa_phi_w1, nsa_phi_w2, nsa_w_br, w_out,
           ln1_g, ln1_b, ln2_g, ln2_b, router_w, router_b, moe_w_gu, moe_b_gu, moe_w_dn, moe_b_dn):
    w_in_p, w_uq_p, w_uk_t, w_uv_t, w1_flat, w1_pair, w2bd = _prep_weights(
        w_in, mla_w_uq, mla_w_uk, mla_w_uv, nsa_phi_w1, nsa_phi_w2)
    pos = jnp.concatenate([jnp.tile(jnp.arange(SEQ), BATCH),
                           jnp.tile(PAST_LEN + jnp.arange(DEC_SEQ), DEC_BATCH)])
    q_pos_s = PAST_LEN + jnp.arange(DEC_SEQ)
    nbs_s = -(-(PAST_LEN + DEC_SEQ) // NSA_BLK)
    mp = M_PROMPT
    x = jnp.concatenate([x_prompt.reshape(mp, D_MODEL), x_sample.reshape(M_SAMPLE, D_MODEL)], axis=0)
    cache_mla_t = jnp.swapaxes(cache_mla, 2, 3)
    st_p, st_s = [], []
    for l in range(DEPTH):
        h = _mm(x, w_in_p[l], tm=512, tn=1536, name=f"in_proj_l{l}")
        cq = _rms_norm(h[:, IN_CQ:IN_CQ + MLA_Q_LORA], mla_q_norm[l])
        q = _mm(cq, w_uq_p[l], tm=1088, tn=1536, name=f"mla_uq_l{l}")
        q_lat = _head_up(q[:, :MLA_HEADS * MLA_NOPE], w_uk_t[l], tm=1088, out_dtype=BF16, name=f"mla_uk_l{l}",
                         scale=MLA_SCALE)
        q_rope = _rope(q[:, MLA_HEADS * MLA_NOPE:].reshape(M_ALL, MLA_HEADS, MLA_ROPE), pos, MLA_ROPE)
        q_rope = jnp.transpose(q_rope * MLA_SCALE, (1, 0, 2)).astype(BF16)
        mla_rows = jnp.concatenate([_rms_norm(h[:, IN_CKV:IN_CKV + MLA_KV_LORA], mla_kv_norm[l]),
                                    _rope(h[:, IN_KR:IN_KR + MLA_ROPE], pos, MLA_ROPE)], axis=1)
        mla_rows_p = mla_rows[:mp].reshape(BATCH, SEQ, MLA_CACHE)
        mla_rows_s = mla_rows[mp:].reshape(DEC_BATCH, DEC_SEQ, MLA_CACHE)
        o_lat_p = _mla_prompt(q_lat[:, :mp].reshape(MLA_HEADS, BATCH, SEQ, MLA_KV_LORA),
                              q_rope[:, :mp].reshape(MLA_HEADS, BATCH, SEQ, MLA_ROPE), mla_rows_p)

        def to_rows(a):
            dd = a.shape[-1]
            return jnp.transpose(a.reshape(MLA_HEADS, DEC_BATCH, DEC_SEQ, dd), (1, 2, 0, 3)).reshape(
                DEC_BATCH, DEC_SEQ * MLA_HEADS, dd)

        o_lat_s = _mla_sample(page_table, to_rows(q_lat[:, mp:]), to_rows(q_rope[:, mp:]),
                              jnp.swapaxes(_pad_rows(mla_rows_s, PAGE_SIZE), 1, 2), cache_mla_t, l)
        o_lat_s = jnp.transpose(o_lat_s.reshape(DEC_BATCH, DEC_SEQ, MLA_HEADS, MLA_KV_LORA),
                                (2, 0, 1, 3)).reshape(MLA_HEADS, M_SAMPLE, MLA_KV_LORA)
        o_lat = jnp.concatenate([o_lat_p.reshape(MLA_HEADS, mp, MLA_KV_LORA), o_lat_s], axis=1)
        o_mla = _head_down(o_lat, w_uv_t[l], tm=1088, out_dtype=BF16, name=f"mla_uv_l{l}")
        y_a = _mm(o_mla, mla_w_br[l].astype(BF16), tm=1088, tn=1024, name=f"mla_br_l{l}")
        glu = h[:, IN_GLU:IN_GLU + 2 * CONV_DIM]
        u = glu[:, :CONV_DIM] * jax.nn.sigmoid(glu[:, CONV_DIM:])
        u_p = u[:mp].reshape(BATCH, SEQ, CONV_DIM)
        u_s = u[mp:].reshape(DEC_BATCH, DEC_SEQ, CONV_DIM)
        c_p = _conv_prompt(u_p, conv_w[l], conv_b[l], conv_ln_g[l], conv_ln_b[l])
        u_ext = jnp.concatenate([state_conv[:, l], u_s], axis=1)
        y_s = sum(u_ext[:, k:k + DEC_SEQ] * conv_w[l][k] for k in range(CONV_WIDTH)) + conv_b[l]
        c_s = jax.nn.silu(_layer_norm(y_s, conv_ln_g[l], conv_ln_b[l])).astype(BF16)
        c_all = jnp.concatenate([c_p.reshape(mp, CONV_DIM), c_s.reshape(M_SAMPLE, CONV_DIM)], axis=0)
        y_conv = _mm(c_all, conv_w_br[l].astype(BF16), tm=1088, tn=1024, name=f"conv_br_l{l}")
        qn = h[:, IN_QN:IN_QN + NSA_HEADS * NSA_DH]
        qr = _rope(qn.reshape(M_ALL, NSA_HEADS, NSA_DH), pos, NSA_ROT) * NSA_SCALE
        kv_cmp = h[:, IN_CMP:IN_CMP + 2 * NSA_DH]
        kv_sel = h[:, IN_SEL:IN_SEL + 2 * NSA_DH]
        kv_sel = jnp.concatenate([_rope(kv_sel[:, :NSA_DH], pos, NSA_ROT), kv_sel[:, NSA_DH:]], axis=1)
        kv_win = h[:, IN_WIN:IN_WIN + 2 * NSA_DH]
        kv_win = jnp.concatenate([_rope(kv_win[:, :NSA_DH], pos, NSA_ROT), kv_win[:, NSA_DH:]], axis=1)
        g_nsa = jax.nn.sigmoid(h[:, IN_GN:IN_GN + 3 * NSA_HEADS]).reshape(M_ALL, NSA_HEADS, 3)
        pe2 = nsa_phi_pe[l].reshape(NSA_BLK, 2 * NSA_DH)
        kv_cmp_p = kv_cmp[:mp].reshape(BATCH, SEQ, 2 * NSA_DH)
        nb_p = SEQ // NSA_BLK
        flat = (kv_cmp_p.reshape(BATCH, nb_p, NSA_BLK, 2 * NSA_DH) + pe2).reshape(BATCH * nb_p, -1)
        hid = jax.nn.gelu(_mm(flat, w1_flat[l], tm=BATCH * nb_p, tn=2 * NSA_PHI_HID, name=f"nsa_phi1_l{l}"))
        kcvc_p = _mm(hid, w2bd[l], tm=BATCH * nb_p, tn=2 * NSA_DH, name=f"nsa_phi2_l{l}")
        o_c_p, selmask_p = _nsa_cmp_prompt(qn[:mp].reshape(BATCH, SEQ, -1),
                                           kcvc_p.reshape(BATCH, nb_p, 2 * NSA_DH))
        qr_p = qr[:mp].reshape(BATCH, SEQ, NSA_HEADS * NSA_DH)
        kv_sel_p = kv_sel[:mp].reshape(BATCH, SEQ, 2 * NSA_DH)
        kv_win_p = kv_win[:mp].reshape(BATCH, SEQ, 2 * NSA_DH)
        o_s_p = _nsa_sel_prompt(qr_p, kv_sel_p, selmask_p)
        o_w_p = _nsa_win_prompt(qr_p, kv_win_p)
        to_tok = lambda a: a.reshape(mp, NSA_HEADS, NSA_DH)
        o_c_p, o_s_p, o_w_p = to_tok(o_c_p), to_tok(o_s_p), to_tok(o_w_p)
        kv_sel_s = kv_sel[mp:].reshape(DEC_BATCH, DEC_SEQ, 2 * NSA_DH)
        kv_win_s = kv_win[mp:].reshape(DEC_BATCH, DEC_SEQ, 2 * NSA_DH)
        qn_s = qn[mp:].reshape(DEC_BATCH, DEC_SEQ * NSA_HEADS, NSA_DH).astype(BF16)
        qr_s = qr[mp:].reshape(DEC_BATCH, DEC_SEQ * NSA_HEADS, NSA_DH).astype(BF16)
        o_c_s, imp_s = _nsa_cmp_sample(page_table, qn_s, pe2, w1_pair[l], w2bd[l], cache_nsa_cmp, l)
        selmask_s = _select_blocks(imp_s, q_pos_s, nbs_s)
        key_pos = jnp.arange(PAST_LEN + PAGE_SIZE)
        allowed = jnp.repeat(selmask_s[:, :, :nbs_s - 1], NSA_BLK, axis=2)
        new_ok = selmask_s[:, :, nbs_s - 1:] * (key_pos[None, None, PAST_LEN:] <= q_pos_s[None, :, None])
        allowed = jnp.concatenate([allowed, new_ok.astype(F32)], axis=2)
        o_s_s, o_w_s = _nsa_selwin_sample(page_table, qr_s, allowed, _pad_rows(kv_sel_s, PAGE_SIZE),
                                          state_nsa_win, _pad_rows(kv_win_s, PAGE_SIZE), cache_nsa_sel, l)
        cat = lambda a, b: jnp.concatenate([a, b.reshape(M_SAMPLE, NSA_HEADS, NSA_DH)], axis=0)
        o_nsa = (g_nsa[..., 0:1] * cat(o_c_p, o_c_s) + g_nsa[..., 1:2] * cat(o_s_p, o_s_s)
                 + g_nsa[..., 2:3] * cat(o_w_p, o_w_s)).reshape(M_ALL, NSA_HEADS * NSA_DH)
        y_n = _mm(o_nsa, nsa_w_br[l].astype(BF16), tm=1088, tn=1024, name=f"nsa_br_l{l}")
        gm = jax.nn.sigmoid(h[:, IN_GM:IN_GM + 3 * D_MODEL])
        m = gm[:, :D_MODEL] * y_a + gm[:, D_MODEL:2 * D_MODEL] * y_conv + gm[:, 2 * D_MODEL:] * y_n
        mix = _mm(m, w_out[l].astype(BF16), tm=1088, tn=1024, name=f"w_out_l{l}")
        x1 = _layer_norm(DN_ALPHA * x + mix, ln1_g[l], ln1_b[l])
        f = _moe(x1, router_w[l], router_b[l], moe_w_gu[l].astype(BF16), moe_b_gu[l],
                 moe_w_dn[l].astype(BF16), moe_b_dn[l], l)
        x = _layer_norm(DN_ALPHA * x1 + f, ln2_g[l], ln2_b[l])
        win_all = jnp.concatenate([state_nsa_win[:, l], kv_win_s], axis=1)
        st_p.append((mla_rows_p, kv_cmp_p, kv_sel_p, kv_win_p[:, -min(NSA_WINDOW, SEQ):],
                     u_p[:, -(CONV_WIDTH - 1):]))
        st_s.append((mla_rows_s, kv_cmp[mp:].reshape(DEC_BATCH, DEC_SEQ, 2 * NSA_DH), kv_sel_s,
                     win_all[:, -state_nsa_win.shape[2]:], u_ext[:, -(CONV_WIDTH - 1):]))
    stack = lambda sts, i: jnp.stack([s[i] for s in sts], axis=1)
    return (x[:mp].reshape(BATCH, SEQ, D_MODEL), x[mp:].reshape(DEC_BATCH, DEC_SEQ, D_MODEL),
            stack(st_p, 0), stack(st_p, 1), stack(st_p, 2), stack(st_p, 3), stack(st_p, 4),
            stack(st_s, 0), stack(st_s, 1), stack(st_s, 2), stack(st_s, 3), stack(st_s, 4))
```

```python
import functools

import numpy as np
import jax
import jax.numpy as jnp
from jax import lax
from jax.experimental import pallas as pl
from jax.experimental.pallas import tpu as pltpu

F32 = jnp.float32
BF16 = jnp.bfloat16

D_MODEL = 2048
BATCH = 2
SEQ = 4096
DEPTH = 4
DEC_BATCH = 128
DEC_SEQ = 4
PAST_LEN = 8192
PAGE_SIZE = 128
N_PAGES = PAST_LEN // PAGE_SIZE

MLA_HEADS = 16
MLA_NOPE = 64
MLA_ROPE = 32
MLA_V = 64
MLA_Q_LORA = 384
MLA_KV_LORA = 256
MLA_CACHE = MLA_KV_LORA + MLA_ROPE
MLA_SCALE = (MLA_NOPE + MLA_ROPE) ** -0.5
CONV_DIM = 1024
CONV_WIDTH = 31
NSA_HEADS = 16
NSA_DH = 64
NSA_ROT = NSA_DH // 4
NSA_BLK = 64
NSA_TOPN = 16
NSA_WINDOW = 512
NSA_PHI_HID = 64
NSA_SCALE = NSA_DH ** -0.5
N_EXPERTS = 32
TOP_K = 4
D_FF = 768
SWIGLU_LIMIT = 7.0
SWIGLU_ALPHA = 1.702
ROPE_THETA = 500000.0
Q_BLOCK = 128
DN_ALPHA = (2 * DEPTH) ** 0.25
NEG_INF = -1e30
FORCE_SCORE = 1e6
RMS_EPS = 1e-6
LN_EPS = 1e-5

M_PROMPT = BATCH * SEQ
M_SAMPLE = DEC_BATCH * DEC_SEQ
M_ALL = M_PROMPT + M_SAMPLE

V7X_VMEM_LIMIT_CAP = 60000 * 1024
LANE = 128

IN_GM = 0
IN_CQ = 6144
IN_CKV = 6528
IN_KR = 6784
IN_GLU = 6912
IN_QN = 8960
IN_CMP = 9984
IN_SEL = 10112
IN_WIN = 10240
IN_GN = 10368
IN_PAD_N = 10752

MOE_TM = 256
MOE_BLOCKS = (M_ALL * TOP_K) // MOE_TM + N_EXPERTS
MOE_SLOTS = MOE_BLOCKS * MOE_TM

CMP_PITCH = 72
HEAD_SHIFT = 4
BLK_SHIFT = 6
assert (1 << HEAD_SHIFT) == MLA_HEADS == NSA_HEADS and (1 << BLK_SHIFT) == NSA_BLK
assert NSA_SCALE == 2.0 ** -3


def _cparams(sem, est_bytes):
    limit = int(min(max(est_bytes * 5 // 4 + (4 << 20), 32 << 20), V7X_VMEM_LIMIT_CAP))
    return pltpu.CompilerParams(dimension_semantics=sem, vmem_limit_bytes=limit)


def _dot(a, b):
    return jnp.dot(a, b, preferred_element_type=F32)


def _dot_nt(a, b):
    return lax.dot_general(a, b, (((1,), (1,)), ((), ())), preferred_element_type=F32)


def _mm_kernel(x_ref, w_ref, o_ref):
    o_ref[...] = _dot(x_ref[...].astype(BF16), w_ref[...].astype(BF16)).astype(o_ref.dtype)


def _mm(x, w, *, tm, tn, out_dtype=F32, name):
    m, k = x.shape
    n = w.shape[1]
    est = 2 * (tm * k * x.dtype.itemsize + k * tn * w.dtype.itemsize + tm * tn * 4) + tm * k * 2 + k * tn * 2
    return pl.pallas_call(
        _mm_kernel,
        out_shape=jax.ShapeDtypeStruct((m, n), out_dtype),
        grid=(pl.cdiv(n, tn), pl.cdiv(m, tm)),
        in_specs=[pl.BlockSpec((tm, k), lambda j, i: (i, 0)),
                  pl.BlockSpec((k, tn), lambda j, i: (0, j))],
        out_specs=pl.BlockSpec((tm, tn), lambda j, i: (i, j)),
        compiler_params=_cparams(("arbitrary", "arbitrary"), est),
        name=name,
    )(x, w)


def _router_kernel(x_ref, w_ref, b_ref, v_ref, e_ref):
    logits = jnp.dot(x_ref[...], w_ref[...], preferred_element_type=F32,
                     precision=lax.Precision.HIGHEST) + b_ref[...]
    tm, ne = logits.shape
    col = lax.broadcasted_iota(jnp.int32, (tm, ne), 1)
    vals = jnp.zeros((tm, ne), F32)
    idxs = jnp.zeros((tm, ne), jnp.int32)
    for k in range(TOP_K):
        m = jnp.max(logits, axis=1, keepdims=True)
        first = jnp.min(jnp.where(logits == m, col, ne), axis=1, keepdims=True)
        vals = jnp.where(col == k, m, vals)
        idxs = jnp.where(col == k, first, idxs)
        logits = jnp.where(col == first, -jnp.inf, logits)
    v_ref[...] = vals
    e_ref[...] = idxs


def _router(x, w, b, *, tm, name):
    m, k = x.shape
    n = w.shape[1]
    est = 2 * (tm * k * 4 + k * LANE * 4 + 2 * tm * LANE * 4) + 8 * tm * LANE * 4
    vals, idxs = pl.pallas_call(
        _router_kernel,
        out_shape=(jax.ShapeDtypeStruct((m, n), F32), jax.ShapeDtypeStruct((m, n), jnp.int32)),
        grid=(pl.cdiv(m, tm),),
        in_specs=[pl.BlockSpec((tm, k), lambda i: (i, 0)),
                  pl.BlockSpec((k, n), lambda i: (0, 0)),
                  pl.BlockSpec((1, n), lambda i: (0, 0))],
        out_specs=(pl.BlockSpec((tm, n), lambda i: (i, 0)), pl.BlockSpec((tm, n), lambda i: (i, 0))),
        compiler_params=_cparams(("arbitrary",), est),
        name=name,
    )(x, w, b.reshape(1, n))
    return vals[:, :TOP_K], idxs[:, :TOP_K]


def _mix_out_kernel(ya_ref, yc_ref, yn_ref, g0_ref, g1_ref, g2_ref, x_ref, w_ref, lg_ref, lb_ref, o_ref):
    m = (jax.nn.sigmoid(g0_ref[...]) * ya_ref[...] + jax.nn.sigmoid(g1_ref[...]) * yc_ref[...]
         + jax.nn.sigmoid(g2_ref[...]) * yn_ref[...])
    z = DN_ALPHA * x_ref[...] + _dot(m.astype(BF16), w_ref[...])
    zc = z - jnp.mean(z, axis=1, keepdims=True)
    var = jnp.mean(zc * zc, axis=1, keepdims=True)
    o_ref[...] = zc * lax.rsqrt(var + LN_EPS) * lg_ref[...] + lb_ref[...]


def _mix_out(y_a, y_conv, y_n, h, x, w_out, ln_g, ln_b, *, tm, name):
    m = x.shape[0]
    row = pl.BlockSpec((tm, D_MODEL), lambda i: (i, 0))
    gate = lambda j: pl.BlockSpec((tm, D_MODEL), lambda i: (i, IN_GM // D_MODEL + j))
    vec = pl.BlockSpec((1, D_MODEL), lambda i: (0, 0))
    est = 2 * 8 * tm * D_MODEL * 4 + D_MODEL * D_MODEL * 2 + 4 * tm * D_MODEL * 4
    return pl.pallas_call(
        _mix_out_kernel,
        out_shape=jax.ShapeDtypeStruct((m, D_MODEL), F32),
        grid=(pl.cdiv(m, tm),),
        in_specs=[row, row, row, gate(0), gate(1), gate(2), row,
                  pl.BlockSpec((D_MODEL, D_MODEL), lambda i: (0, 0), pipeline_mode=pl.Buffered(1)),
                  vec, vec],
        out_specs=row,
        compiler_params=_cparams(("arbitrary",), est),
        name=name,
    )(y_a, y_conv, y_n, h, h, h, x, w_out, ln_g.reshape(1, -1), ln_b.reshape(1, -1))


def _moe_combine_kernel(x_ref, a_ref, b_ref, c_ref, d_ref, lg_ref, lb_ref, o_ref):
    z = DN_ALPHA * x_ref[...] + ((a_ref[...] + b_ref[...]) + (c_ref[...] + d_ref[...]))
    zc = z - jnp.mean(z, axis=1, keepdims=True)
    var = jnp.mean(zc * zc, axis=1, keepdims=True)
    o_ref[...] = zc * lax.rsqrt(var + LN_EPS) * lg_ref[...] + lb_ref[...]


def _moe_combine(x, parts, ln_g, ln_b, *, tm, name):
    m = x.shape[0]
    row = pl.BlockSpec((tm, D_MODEL), lambda i: (i, 0))
    vec = pl.BlockSpec((1, D_MODEL), lambda i: (0, 0))
    est = 2 * 6 * tm * D_MODEL * 4 + 4 * tm * D_MODEL * 4
    return pl.pallas_call(
        _moe_combine_kernel,
        out_shape=jax.ShapeDtypeStruct((m, D_MODEL), F32),
        grid=(pl.cdiv(m, tm),),
        in_specs=[row] * (1 + TOP_K) + [vec, vec],
        out_specs=row,
        compiler_params=_cparams(("arbitrary",), est),
        name=name,
    )(x, *parts, ln_g.reshape(1, -1), ln_b.reshape(1, -1))


def _head_up_kernel(x_ref, w_ref, o_ref, *, scale):
    kh = w_ref.shape[1]
    x = x_ref[...].astype(BF16)
    for j in range(2):
        o_ref[j] = (_dot(x[:, j * kh:(j + 1) * kh], w_ref[j]) * scale).astype(o_ref.dtype)


def _head_up(x, w, *, tm, out_dtype, name, scale=1.0):
    m = x.shape[0]
    nheads, kh, nh = w.shape
    est = 2 * (tm * 2 * kh * 4 + 2 * kh * nh * 2 + 2 * tm * nh * 4)
    return pl.pallas_call(
        functools.partial(_head_up_kernel, scale=scale),
        out_shape=jax.ShapeDtypeStruct((nheads, m, nh), out_dtype),
        grid=(nheads // 2, pl.cdiv(m, tm)),
        in_specs=[pl.BlockSpec((tm, 2 * kh), lambda h, i: (i, h)),
                  pl.BlockSpec((2, kh, nh), lambda h, i: (h, 0, 0))],
        out_specs=pl.BlockSpec((2, tm, nh), lambda h, i: (h, i, 0)),
        compiler_params=_cparams(("arbitrary", "arbitrary"), est),
        name=name,
    )(x, w)


def _head_down_kernel(x_ref, w_ref, o_ref):
    o_ref[...] = jnp.concatenate(
        [_dot(x_ref[j].astype(BF16), w_ref[j]) for j in range(2)], axis=1).astype(o_ref.dtype)


def _head_down(x, w, *, tm, out_dtype, name):
    nheads, m, kh = x.shape
    nh = w.shape[2]
    est = 2 * (2 * tm * kh * 4 + 2 * kh * nh * 2 + tm * 2 * nh * 4)
    return pl.pallas_call(
        _head_down_kernel,
        out_shape=jax.ShapeDtypeStruct((m, nheads * nh), out_dtype),
        grid=(nheads // 2, pl.cdiv(m, tm)),
        in_specs=[pl.BlockSpec((2, tm, kh), lambda h, i: (h, i, 0)),
                  pl.BlockSpec((2, kh, nh), lambda h, i: (h, 0, 0))],
        out_specs=pl.BlockSpec((tm, 2 * nh), lambda h, i: (i, h)),
        compiler_params=_cparams(("arbitrary", "arbitrary"), est),
        name=name,
    )(x, w)


def _softmax_init(m_sc, l_sc, acc_sc):
    m_sc[...] = jnp.full(m_sc.shape, NEG_INF, F32)
    l_sc[...] = jnp.zeros(l_sc.shape, F32)
    acc_sc[...] = jnp.zeros(acc_sc.shape, F32)


def _softmax_update(s, v, m_sc, l_sc, acc_sc, *, v_transposed=False):
    m_prev = m_sc[...]
    m_new = jnp.maximum(m_prev, jnp.max(s, axis=1, keepdims=True))
    alpha = jnp.exp(m_prev - m_new)
    p = jnp.exp(s - m_new)
    pb = p.astype(BF16)
    pv = _dot_nt(pb, v) if v_transposed else _dot(pb, v)
    l_sc[...] = alpha * l_sc[...] + jnp.sum(p, axis=1, keepdims=True)
    acc_sc[...] = alpha * acc_sc[...] + pv
    m_sc[...] = m_new


def _lanes(x, n):
    return x[:, :n] if n <= LANE else jnp.tile(x, (1, n // LANE))


def _flash_rows(s_sc, p_sc, bias, m_sc, l_sc, acc_sc, *, group_rows):
    rows, keys = s_sc.shape
    dv = acc_sc.shape[1]
    s = s_sc[...]
    if bias is not None:
        s = _add_group_bias(s, bias, rows // group_rows)
    m_prev = m_sc[...]
    m_new = jnp.maximum(m_prev, jnp.max(s, axis=1, keepdims=True))
    alpha = jnp.exp(m_prev - m_new)
    p = jnp.exp(s - _lanes(m_new, keys))
    l_sc[...] = alpha * l_sc[...] + jnp.sum(p, axis=1, keepdims=True)
    acc_sc[...] = _lanes(alpha, dv) * acc_sc[...]
    m_sc[...] = m_new
    p_sc[...] = p.astype(BF16)


def _add_group_bias(s, bias, n_groups):
    g, keys = bias.shape
    return (s.reshape(n_groups, g, keys) + bias[None]).reshape(n_groups * g, keys)


def _mla_prompt_kernel(ql_ref, qr_ref, kv_ref, o_ref, m_sc, l_sc, acc_sc, s_sc, p_sc, *, tk):
    qi = pl.program_id(1)
    kb = pl.program_id(2)
    rows = MLA_HEADS * Q_BLOCK
    last = ((qi + 1) * Q_BLOCK - 1) // tk
    visible = (kb + 1) * tk <= qi * Q_BLOCK

    def tile(bias):
        ql = ql_ref[...].reshape(rows, MLA_KV_LORA)
        qr = qr_ref[...].reshape(rows, MLA_ROPE)
        kv = kv_ref[...]
        c = kv[:, :MLA_KV_LORA].astype(BF16)
        kr = kv[:, MLA_KV_LORA:].astype(BF16)
        s_sc[...] = _dot_nt(ql, c) + _dot_nt(qr, kr)
        _flash_rows(s_sc, p_sc, bias, m_sc, l_sc, acc_sc, group_rows=Q_BLOCK)
        acc_sc[...] += _dot(p_sc[...], c)

    @pl.when(kb == 0)
    def _():
        _softmax_init(m_sc, l_sc, acc_sc)

    @pl.when(visible)
    def _():
        tile(None)

    @pl.when(jnp.logical_and(kb <= last, jnp.logical_not(visible)))
    def _():
        qpos = qi * Q_BLOCK + lax.broadcasted_iota(jnp.int32, (Q_BLOCK, tk), 0)
        kpos = kb * tk + lax.broadcasted_iota(jnp.int32, (Q_BLOCK, tk), 1)
        tile(jnp.where(kpos <= qpos, 0.0, NEG_INF))

    @pl.when(kb == pl.num_programs(2) - 1)
    def _():
        o = acc_sc[...] / l_sc[:, :1]
        o_ref[...] = o.reshape(MLA_HEADS, Q_BLOCK, MLA_KV_LORA).astype(o_ref.dtype)


def _mla_prompt(ql, qr, kv, *, tk=512):
    nq = SEQ // Q_BLOCK
    nk = SEQ // tk
    rows = MLA_HEADS * Q_BLOCK

    def kv_map(b, qi, kb):
        return (b, jnp.minimum(kb, ((qi + 1) * Q_BLOCK - 1) // tk), 0)

    est = (2 * (rows * MLA_KV_LORA * 2 + rows * LANE * 2 + tk * 3 * LANE * 4 + rows * MLA_KV_LORA * 2)
           + rows * (MLA_KV_LORA + 2 * LANE) * 4 + 3 * rows * tk * 4 + rows * MLA_KV_LORA * 4)
    return pl.pallas_call(
        functools.partial(_mla_prompt_kernel, tk=tk),
        out_shape=jax.ShapeDtypeStruct((MLA_HEADS, BATCH, SEQ, MLA_KV_LORA), BF16),
        grid=(BATCH, nq, nk),
        in_specs=[pl.BlockSpec((MLA_HEADS, None, Q_BLOCK, MLA_KV_LORA), lambda b, qi, kb: (0, b, qi, 0)),
                  pl.BlockSpec((MLA_HEADS, None, Q_BLOCK, MLA_ROPE), lambda b, qi, kb: (0, b, qi, 0)),
                  pl.BlockSpec((None, tk, MLA_CACHE), kv_map)],
        out_specs=pl.BlockSpec((MLA_HEADS, None, Q_BLOCK, MLA_KV_LORA), lambda b, qi, kb: (0, b, qi, 0)),
        scratch_shapes=[pltpu.VMEM((rows, LANE), F32), pltpu.VMEM((rows, LANE), F32),
                        pltpu.VMEM((rows, MLA_KV_LORA), F32), pltpu.VMEM((rows, tk), F32),
                        pltpu.VMEM((rows, tk), BF16)],
        compiler_params=_cparams(("arbitrary", "arbitrary", "arbitrary"), est),
        name="mla_prompt",
    )(ql, qr, kv)


def _mla_sample_kernel(pt_ref, ql_ref, qr_ref, new_ref, *rest, pp):
    del pt_ref
    pages = rest[:pp]
    o_ref, m_sc, l_sc, acc_sc = rest[pp:]
    ch = pl.program_id(1)
    rows = DEC_SEQ * MLA_HEADS

    @pl.when(ch == 0)
    def _():
        _softmax_init(m_sc, l_sc, acc_sc)

    ql = ql_ref[...]
    qr = qr_ref[...]
    kvt = jnp.concatenate([p[...] for p in pages], axis=1)
    ct = kvt[:MLA_KV_LORA].astype(BF16)
    krt = kvt[MLA_KV_LORA:].astype(BF16)
    s = _dot(ql, ct) + _dot(qr, krt)
    _softmax_update(s, ct, m_sc, l_sc, acc_sc, v_transposed=True)

    @pl.when(ch == pl.num_programs(1) - 1)
    def _():
        kvn = new_ref[...]
        cn = kvn[:MLA_KV_LORA].astype(BF16)
        krn = kvn[MLA_KV_LORA:].astype(BF16)
        sn = _dot(ql, cn) + _dot(qr, krn)
        t = lax.broadcasted_iota(jnp.int32, (rows, PAGE_SIZE), 0) >> HEAD_SHIFT
        j = lax.broadcasted_iota(jnp.int32, (rows, PAGE_SIZE), 1)
        sn = jnp.where(j <= t, sn, NEG_INF)
        _softmax_update(sn, cn, m_sc, l_sc, acc_sc, v_transposed=True)
        o_ref[...] = (acc_sc[...] / l_sc[...]).astype(o_ref.dtype)


def _mla_sample(page_table, ql, qr, new_rows, cache, layer, *, pp=32):
    rows = DEC_SEQ * MLA_HEADS
    nch = N_PAGES // pp

    def page_spec(p):
        return pl.BlockSpec((None, None, MLA_CACHE, PAGE_SIZE),
                            lambda b, ch, pt: (pt[b, ch * pp + p], layer, 0, 0))

    est = (2 * (pp + 1) * PAGE_SIZE * 3 * LANE * 4 + 3 * pp * PAGE_SIZE * 3 * LANE * 4
           + 6 * rows * pp * PAGE_SIZE * 4)
    grid_spec = pltpu.PrefetchScalarGridSpec(
        num_scalar_prefetch=1,
        grid=(DEC_BATCH, nch),
        in_specs=[pl.BlockSpec((None, rows, MLA_KV_LORA), lambda b, ch, pt: (b, 0, 0)),
                  pl.BlockSpec((None, rows, MLA_ROPE), lambda b, ch, pt: (b, 0, 0)),
                  pl.BlockSpec((None, MLA_CACHE, PAGE_SIZE), lambda b, ch, pt: (b, 0, 0))]
        + [page_spec(p) for p in range(pp)],
        out_specs=pl.BlockSpec((None, rows, MLA_KV_LORA), lambda b, ch, pt: (b, 0, 0)),
        scratch_shapes=[pltpu.VMEM((rows, 1), F32), pltpu.VMEM((rows, 1), F32),
                        pltpu.VMEM((rows, MLA_KV_LORA), F32)],
    )
    return pl.pallas_call(
        functools.partial(_mla_sample_kernel, pp=pp),
        out_shape=jax.ShapeDtypeStruct((DEC_BATCH, rows, MLA_KV_LORA), BF16),
        grid_spec=grid_spec,
        compiler_params=_cparams(("arbitrary", "arbitrary"), est),
        name="mla_sample",
    )(page_table, ql, qr, new_rows, *([cache] * pp))


def _conv_prompt_kernel(prev_ref, cur_ref, w_ref, b_ref, g_ref, beta_ref, o_ref, ext_sc, *, tm):
    i = pl.program_id(1)
    halo = 32
    sub = 32
    prev = prev_ref[...]
    ext_sc[0:halo, :] = jnp.where(i > 0, prev, 0.0)
    ext_sc[halo:halo + tm, :] = cur_ref[...]
    for r0 in range(0, tm, sub):
        acc = jnp.zeros((sub, CONV_DIM), F32)
        for k in range(CONV_WIDTH):
            acc = acc + ext_sc[pl.ds(r0 + halo - (CONV_WIDTH - 1) + k, sub), :] * w_ref[k:k + 1, :]
        y = acc + b_ref[...]
        mu = jnp.mean(y, axis=1, keepdims=True)
        yc = y - mu
        var = jnp.mean(yc * yc, axis=1, keepdims=True)
        z = yc * lax.rsqrt(var + LN_EPS) * g_ref[...] + beta_ref[...]
        o_ref[r0:r0 + sub, :] = (z * jax.nn.sigmoid(z)).astype(o_ref.dtype)


def _conv_prompt(u, w, b, g, beta, *, tm=256):
    halo = 32
    est = 2 * (halo + 2 * tm) * CONV_DIM * 4 + (halo + 6 * tm) * CONV_DIM * 4
    vec = pl.BlockSpec((1, CONV_DIM), lambda bb, i: (0, 0))
    return pl.pallas_call(
        functools.partial(_conv_prompt_kernel, tm=tm),
        out_shape=jax.ShapeDtypeStruct((BATCH, SEQ, CONV_DIM), BF16),
        grid=(BATCH, SEQ // tm),
        in_specs=[pl.BlockSpec((None, halo, CONV_DIM),
                               lambda bb, i: (bb, jnp.maximum(i * (tm // halo) - 1, 0), 0)),
                  pl.BlockSpec((None, tm, CONV_DIM), lambda bb, i: (bb, i, 0)),
                  pl.BlockSpec((CONV_WIDTH, CONV_DIM), lambda bb, i: (0, 0)),
                  vec, vec, vec],
        out_specs=pl.BlockSpec((None, tm, CONV_DIM), lambda bb, i: (bb, i, 0)),
        scratch_shapes=[pltpu.VMEM((halo + tm, CONV_DIM), F32)],
        compiler_params=_cparams(("arbitrary", "arbitrary"), est),
        name="conv_prompt",
    )(u, u, w, b.reshape(1, -1), g.reshape(1, -1), beta.reshape(1, -1))


def _masked_softmax_rows(s, mask):
    s = jnp.where(mask, s, NEG_INF)
    m = jnp.max(s, axis=1, keepdims=True)
    p = jnp.where(mask, jnp.exp(s - m), 0.0)
    l = jnp.sum(p, axis=1, keepdims=True)
    return p / jnp.where(l > 0.0, l, 1.0)


def _select_top_blocks(imp, qpos):
    rows, nbs = imp.shape
    blk = lax.broadcasted_iota(jnp.int32, (rows, nbs), 1)
    cur = qpos >> BLK_SHIFT
    forced = (blk == 0) | (blk == cur) | (blk == cur - 1)
    score = jnp.where(blk > cur, -1.0, jnp.where(forced, FORCE_SCORE, imp))
    sel = jnp.zeros((rows, nbs), F32)
    for _ in range(min(NSA_TOPN, nbs)):
        m = jnp.max(score, axis=1, keepdims=True)
        first = jnp.min(jnp.where(score == m, blk, nbs), axis=1, keepdims=True)
        hit = blk == first
        sel = jnp.where(hit & (m >= 0.0), 1.0, sel)
        score = jnp.where(hit, -2.0, score)
    return sel


def _nsa_cmp_prompt_kernel(q_ref, kcvc_ref, oc_ref, sel_ref, *, tq):
    i = pl.program_id(1)
    nb = kcvc_ref.shape[0]
    kcvc = kcvc_ref[...]
    kc = kcvc[:, :NSA_DH].astype(BF16)
    vc = kcvc[:, NSA_DH:].astype(BF16)
    qpos = i * tq + lax.broadcasted_iota(jnp.int32, (tq, nb), 0)
    bend = (lax.broadcasted_iota(jnp.int32, (tq, nb), 1) + 1) * NSA_BLK - 1
    mask = bend <= qpos
    q = q_ref[...].astype(BF16)
    imp = jnp.zeros((tq, nb), F32)
    outs = []
    for h in range(NSA_HEADS):
        s = _dot_nt(q[:, h * NSA_DH:(h + 1) * NSA_DH], kc) * NSA_SCALE
        p = _masked_softmax_rows(s, mask)
        imp = imp + p
        outs.append(_dot(p.astype(BF16), vc))
    oc_ref[...] = jnp.concatenate(outs, axis=1)
    sel_ref[...] = _select_top_blocks(imp, qpos)


def _nsa_cmp_prompt(qn, kcvc, *, tq=256):
    nb = kcvc.shape[1]
    hd = NSA_HEADS * NSA_DH
    est = 2 * (tq * hd * 4 * 2 + nb * LANE * 4 + tq * LANE * 4) + 8 * tq * hd * 4
    return pl.pallas_call(
        functools.partial(_nsa_cmp_prompt_kernel, tq=tq),
        out_shape=(jax.ShapeDtypeStruct((BATCH, SEQ, hd), F32),
                   jax.ShapeDtypeStruct((BATCH, SEQ, nb), F32)),
        grid=(BATCH, SEQ // tq),
        in_specs=[pl.BlockSpec((None, tq, hd), lambda b, i: (b, i, 0)),
                  pl.BlockSpec((None, nb, 2 * NSA_DH), lambda b, i: (b, 0, 0))],
        out_specs=(pl.BlockSpec((None, tq, hd), lambda b, i: (b, i, 0)),
                   pl.BlockSpec((None, tq, nb), lambda b, i: (b, i, 0))),
        compiler_params=_cparams(("arbitrary", "arbitrary"), est),
        name="nsa_cmp_prompt",
    )(qn, kcvc)


def _nsa_cmp_sample_kernel(pt_ref, q_ref, pe_ref, w1_ref, w2_ref, *rest):
    del pt_ref
    pages = rest[:N_PAGES]
    oc_ref, imp_ref, x_sc = rest[N_PAGES:]
    nb = 2 * N_PAGES
    for p in range(N_PAGES):
        pg = pages[p][...]
        x_sc[pl.ds((2 * p) * CMP_PITCH, NSA_BLK), :] = pg[:NSA_BLK]
        x_sc[pl.ds((2 * p + 1) * CMP_PITCH, NSA_BLK), :] = pg[NSA_BLK:]
    acc = jnp.zeros((nb, 2 * NSA_PHI_HID), F32)
    for r2 in range(NSA_BLK // 2):
        a0 = x_sc[pl.ds(2 * r2, nb, stride=CMP_PITCH), :] + pe_ref[2 * r2:2 * r2 + 1, :]
        a1 = x_sc[pl.ds(2 * r2 + 1, nb, stride=CMP_PITCH), :] + pe_ref[2 * r2 + 1:2 * r2 + 2, :]
        a = jnp.concatenate([a0, a1], axis=1).astype(BF16)
        acc = acc + _dot(a, w1_ref[r2])
    hid = jax.nn.gelu(acc)
    kcvc = _dot(hid.astype(BF16), w2_ref[...])
    kc = kcvc[:, :NSA_DH].astype(BF16)
    vc = kcvc[:, NSA_DH:].astype(BF16)
    rows = DEC_SEQ * NSA_HEADS
    s = _dot_nt(q_ref[...], kc) * NSA_SCALE
    qpos = PAST_LEN + (lax.broadcasted_iota(jnp.int32, (rows, nb), 0) >> HEAD_SHIFT)
    bend = (lax.broadcasted_iota(jnp.int32, (rows, nb), 1) + 1) * NSA_BLK - 1
    p = _masked_softmax_rows(s, bend <= qpos)
    oc_ref[...] = _dot(p.astype(BF16), vc)
    imp_ref[...] = jnp.sum(p.reshape(DEC_SEQ, NSA_HEADS, nb), axis=1)


def _nsa_cmp_sample(page_table, qn, pe2, w1p, w2bd, cache, layer):
    rows = DEC_SEQ * NSA_HEADS
    nb = 2 * N_PAGES

    def page_spec(p):
        return pl.BlockSpec((None, None, PAGE_SIZE, 2 * NSA_DH), lambda b, pt: (pt[b, p], layer, 0, 0))

    est = (2 * N_PAGES * PAGE_SIZE * LANE * 4 + nb * CMP_PITCH * LANE * 4
           + 2 * (NSA_BLK // 2) * 2 * LANE * LANE * 2 + (8 << 20))
    grid_spec = pltpu.PrefetchScalarGridSpec(
        num_scalar_prefetch=1,
        grid=(DEC_BATCH,),
        in_specs=[pl.BlockSpec((None, rows, NSA_DH), lambda b, pt: (b, 0, 0)),
                  pl.BlockSpec((NSA_BLK, 2 * NSA_DH), lambda b, pt: (0, 0)),
                  pl.BlockSpec((NSA_BLK // 2, 4 * NSA_DH, 2 * NSA_PHI_HID), lambda b, pt: (0, 0, 0)),
                  pl.BlockSpec((2 * NSA_PHI_HID, 2 * NSA_DH), lambda b, pt: (0, 0))]
        + [page_spec(p) for p in range(N_PAGES)],
        out_specs=(pl.BlockSpec((None, rows, NSA_DH), lambda b, pt: (b, 0, 0)),
                   pl.BlockSpec((None, DEC_SEQ, nb), lambda b, pt: (b, 0, 0))),
        scratch_shapes=[pltpu.VMEM((nb * CMP_PITCH, 2 * NSA_DH), F32)],
    )
    return pl.pallas_call(
        _nsa_cmp_sample_kernel,
        out_shape=(jax.ShapeDtypeStruct((DEC_BATCH, rows, NSA_DH), F32),
                   jax.ShapeDtypeStruct((DEC_BATCH, DEC_SEQ, nb), F32)),
        grid_spec=grid_spec,
        compiler_params=_cparams(("arbitrary",), est),
        name="nsa_cmp_sample",
    )(page_table, qn, pe2, w1p, w2bd, *([cache] * N_PAGES))


def _split_heads(q):
    return jnp.concatenate([q[:, h * NSA_DH:(h + 1) * NSA_DH] for h in range(NSA_HEADS)], axis=0).astype(BF16)


def _merge_heads(o):
    return jnp.concatenate([o[h * Q_BLOCK:(h + 1) * Q_BLOCK] for h in range(NSA_HEADS)], axis=1)


def _nsa_sel_prompt_kernel(q_ref, kv_ref, sel_ref, o_ref, m_sc, l_sc, acc_sc, s_sc, p_sc, *, tk):
    qi = pl.program_id(1)
    kb = pl.program_id(2)
    rows = NSA_HEADS * Q_BLOCK
    last = ((qi + 1) * Q_BLOCK - 1) // tk
    nbs = sel_ref.shape[1]

    @pl.when(kb == 0)
    def _():
        _softmax_init(m_sc, l_sc, acc_sc)

    @pl.when(kb <= last)
    def _():
        q = _split_heads(q_ref[...])
        kv = kv_ref[...]
        k = kv[:, :NSA_DH].astype(BF16)
        v = kv[:, NSA_DH:].astype(BF16)
        kpos_e = kb * tk + lax.broadcasted_iota(jnp.int32, (nbs, tk), 1)
        expand = ((kpos_e >> BLK_SHIFT) == lax.broadcasted_iota(jnp.int32, (nbs, tk), 0)).astype(BF16)
        chosen = _dot(sel_ref[...].astype(BF16), expand)
        qpos = qi * Q_BLOCK + lax.broadcasted_iota(jnp.int32, (Q_BLOCK, tk), 0)
        kpos = kb * tk + lax.broadcasted_iota(jnp.int32, (Q_BLOCK, tk), 1)
        bias = jnp.where((chosen > 0.5) & (kpos <= qpos), 0.0, NEG_INF)
        s_sc[...] = _dot_nt(q, k)
        _flash_rows(s_sc, p_sc, bias, m_sc, l_sc, acc_sc, group_rows=Q_BLOCK)
        acc_sc[...] += _dot(p_sc[...], v)

    @pl.when(kb == pl.num_programs(2) - 1)
    def _():
        o = acc_sc[...] / l_sc[:, :1]
        o_ref[...] = _merge_heads(o)


def _nsa_sel_prompt(qr, kv_sel, selmask, *, tk=512):
    nq = SEQ // Q_BLOCK
    nk = SEQ // tk
    rows = NSA_HEADS * Q_BLOCK
    nbs = selmask.shape[2]

    def kv_map(b, qi, kb):
        return (b, jnp.minimum(kb, ((qi + 1) * Q_BLOCK - 1) // tk), 0)

    est = 2 * (rows * LANE * 2 + tk * LANE * 4 + Q_BLOCK * LANE * 4 + rows * LANE * 4) + 3 * rows * LANE * 4 \
        + 5 * rows * tk * 4
    return pl.pallas_call(
        functools.partial(_nsa_sel_prompt_kernel, tk=tk),
        out_shape=jax.ShapeDtypeStruct((BATCH, SEQ, NSA_HEADS * NSA_DH), F32),
        grid=(BATCH, nq, nk),
        in_specs=[pl.BlockSpec((None, Q_BLOCK, NSA_HEADS * NSA_DH), lambda b, qi, kb: (b, qi, 0)),
                  pl.BlockSpec((None, tk, 2 * NSA_DH), kv_map),
                  pl.BlockSpec((None, Q_BLOCK, nbs), lambda b, qi, kb: (b, qi, 0))],
        out_specs=pl.BlockSpec((None, Q_BLOCK, NSA_HEADS * NSA_DH), lambda b, qi, kb: (b, qi, 0)),
        scratch_shapes=[pltpu.VMEM((rows, LANE), F32), pltpu.VMEM((rows, LANE), F32),
                        pltpu.VMEM((rows, NSA_DH), F32), pltpu.VMEM((rows, tk), F32),
                        pltpu.VMEM((rows, tk), BF16)],
        compiler_params=_cparams(("arbitrary", "arbitrary", "arbitrary"), est),
        name="nsa_sel_prompt",
    )(qr, kv_sel, selmask)


def _nsa_win_prompt_kernel(q_ref, *rest, nw):
    tiles = rest[:nw]
    o_ref = rest[nw]
    qi = pl.program_id(1)
    nk = nw * Q_BLOCK
    q = _split_heads(q_ref[...])
    kv = jnp.concatenate([t[...] for t in tiles], axis=0)
    k = kv[:, :NSA_DH].astype(BF16)
    v = kv[:, NSA_DH:].astype(BF16)
    qpos = qi * Q_BLOCK + lax.broadcasted_iota(jnp.int32, (Q_BLOCK, nk), 0)
    kpos = (qi - (nw - 1)) * Q_BLOCK + lax.broadcasted_iota(jnp.int32, (Q_BLOCK, nk), 1)
    bias = jnp.where((kpos <= qpos) & (kpos > qpos - NSA_WINDOW) & (kpos >= 0), 0.0, NEG_INF)
    s = _add_group_bias(_dot_nt(q, k), bias, NSA_HEADS)
    p = jnp.exp(s - jnp.max(s, axis=1, keepdims=True))
    o = _dot(p.astype(BF16), v) / jnp.sum(p, axis=1, keepdims=True)
    o_ref[...] = _merge_heads(o)


def _nsa_win_prompt(qr, kv_win):
    nq = SEQ // Q_BLOCK
    nw = NSA_WINDOW // Q_BLOCK + 1
    rows = NSA_HEADS * Q_BLOCK
    nk = nw * Q_BLOCK

    def tile_spec(w):
        return pl.BlockSpec((None, Q_BLOCK, 2 * NSA_DH),
                            lambda b, qi: (b, jnp.maximum(qi - (nw - 1) + w, 0), 0))

    est = 2 * (rows * LANE * 2 + nk * LANE * 4 + rows * LANE * 4) + 3 * rows * LANE * 4 + 3 * rows * nk * 4
    return pl.pallas_call(
        functools.partial(_nsa_win_prompt_kernel, nw=nw),
        out_shape=jax.ShapeDtypeStruct((BATCH, SEQ, NSA_HEADS * NSA_DH), F32),
        grid=(BATCH, nq),
        in_specs=[pl.BlockSpec((None, Q_BLOCK, NSA_HEADS * NSA_DH), lambda b, qi: (b, qi, 0))]
        + [tile_spec(w) for w in range(nw)],
        out_specs=pl.BlockSpec((None, Q_BLOCK, NSA_HEADS * NSA_DH), lambda b, qi: (b, qi, 0)),
        compiler_params=_cparams(("arbitrary", "arbitrary"), est),
        name="nsa_win_prompt",
    )(qr, *([kv_win] * nw))


def _nsa_selwin_sample_kernel(pt_ref, q_ref, allowed_ref, selnew_ref, win_ref, winnew_ref, *rest):
    del pt_ref
    pages = rest[:N_PAGES]
    os_ref, ow_ref = rest[N_PAGES:]
    rows = DEC_SEQ * NSA_HEADS
    q = q_ref[...]
    kv = jnp.concatenate([p[...] for p in pages] + [selnew_ref[...]], axis=0)
    nk = kv.shape[0]
    k = kv[:, :NSA_DH].astype(BF16)
    v = kv[:, NSA_DH:].astype(BF16)
    allowed = jnp.broadcast_to(allowed_ref[...][:, None, :], (DEC_SEQ, NSA_HEADS, nk)).reshape(rows, nk)
    p = _masked_softmax_rows(_dot_nt(q, k), allowed > 0.5)
    os_ref[...] = _dot(p.astype(BF16), v)
    kvw = jnp.concatenate([win_ref[...], winnew_ref[...]], axis=0)
    nkw = kvw.shape[0]
    wb = win_ref.shape[0]
    kw = kvw[:, :NSA_DH].astype(BF16)
    vw = kvw[:, NSA_DH:].astype(BF16)
    t = lax.broadcasted_iota(jnp.int32, (rows, nkw), 0) >> HEAD_SHIFT
    j = lax.broadcasted_iota(jnp.int32, (rows, nkw), 1)
    kp = PAST_LEN - wb + j
    qp = PAST_LEN + t
    wmask = (kp <= qp) & (kp > qp - NSA_WINDOW) & (kp >= 0) & (j < wb + DEC_SEQ)
    pw = _masked_softmax_rows(_dot_nt(q, kw), wmask)
    ow_ref[...] = _dot(pw.astype(BF16), vw)


def _nsa_selwin_sample(page_table, qr, allowed, sel_new, win_state, win_new, cache, layer):
    rows = DEC_SEQ * NSA_HEADS
    nk = PAST_LEN + PAGE_SIZE
    wb = win_state.shape[2]

    def page_spec(p):
        return pl.BlockSpec((None, None, PAGE_SIZE, 2 * NSA_DH), lambda b, pt: (pt[b, p], layer, 0, 0))

    est = 2 * (N_PAGES + 2) * PAGE_SIZE * LANE * 4 + 2 * wb * LANE * 4 + 3 * nk * LANE * 4 + 8 * rows * nk * 4
    grid_spec = pltpu.PrefetchScalarGridSpec(
        num_scalar_prefetch=1,
        grid=(DEC_BATCH,),
        in_specs=[pl.BlockSpec((None, rows, NSA_DH), lambda b, pt: (b, 0, 0)),
                  pl.BlockSpec((None, DEC_SEQ, nk), lambda b, pt: (b, 0, 0)),
                  pl.BlockSpec((None, PAGE_SIZE, 2 * NSA_DH), lambda b, pt: (b, 0, 0)),
                  pl.BlockSpec((None, None, wb, 2 * NSA_DH), lambda b, pt: (b, layer, 0, 0)),
                  pl.BlockSpec((None, PAGE_SIZE, 2 * NSA_DH), lambda b, pt: (b, 0, 0))]
        + [page_spec(p) for p in range(N_PAGES)],
        out_specs=(pl.BlockSpec((None, rows, NSA_DH), lambda b, pt: (b, 0, 0)),
                   pl.BlockSpec((None, rows, NSA_DH), lambda b, pt: (b, 0, 0))),
    )
    return pl.pallas_call(
        _nsa_selwin_sample_kernel,
        out_shape=(jax.ShapeDtypeStruct((DEC_BATCH, rows, NSA_DH), F32),
                   jax.ShapeDtypeStruct((DEC_BATCH, rows, NSA_DH), F32)),
        grid_spec=grid_spec,
        compiler_params=_cparams(("arbitrary",), est),
        name="nsa_selwin_sample",
    )(page_table, qr, allowed, sel_new, win_state, win_new, *([cache] * N_PAGES))


def _moe_row_copy(x_hbm, xbuf, sem, tok, slot, r):
    return pltpu.make_async_copy(x_hbm.at[pl.ds(tok, 1)], xbuf.at[slot, pl.ds(r, 1)], sem.at[slot])


def _moe_kernel(be_ref, nu_ref, tok0_ref, tokn_ref, x_hbm, sw_ref, wgu_ref, bgu_ref, wdn_ref, bdn_ref,
                o_ref, xbuf, sem):
    del be_ref
    i = pl.program_id(0)
    n_used = nu_ref[0]

    def start_rows(tok_ref, slot):
        def body(r, carry):
            _moe_row_copy(x_hbm, xbuf, sem, tok_ref[0, r], slot, r).start()
            return carry
        lax.fori_loop(0, MOE_TM, body, 0, unroll=8)

    def wait_rows(slot):
        def body(r, carry):
            _moe_row_copy(x_hbm, xbuf, sem, 0, slot, r).wait()
            return carry
        lax.fori_loop(0, MOE_TM, body, 0, unroll=8)

    @pl.when(i == 0)
    def _():
        start_rows(tok0_ref, 0)

    @pl.when(i + 1 < n_used)
    def _():
        start_rows(tokn_ref, (i + 1) % 2)

    @pl.when(i < n_used)
    def _():
        slot = i % 2
        wait_rows(slot)
        h = _dot(xbuf[slot].astype(BF16), wgu_ref[...]) + bgu_ref[...]
        gate = jnp.minimum(h[:, :D_FF], SWIGLU_LIMIT)
        up = jnp.clip(h[:, D_FF:], -SWIGLU_LIMIT, SWIGLU_LIMIT)
        act = gate * jax.nn.sigmoid(SWIGLU_ALPHA * gate) * (up + 1.0)
        y = _dot(act.astype(BF16), wdn_ref[...]) + bdn_ref[...]
        o_ref[...] = y * sw_ref[...]

    @pl.when(i >= n_used)
    def _():
        o_ref[...] = jnp.zeros(o_ref.shape, F32)


def _moe_experts(blk_e, n_used, slot_tok, x, slot_w, w_gu, b_gu, w_dn, b_dn):
    tm = MOE_TM

    def row_map(i, be, nu):
        return (jnp.minimum(i, nu[0] - 1), 0)

    def exp_map(i, be, nu):
        return (be[jnp.minimum(i, nu[0] - 1)], 0, 0)

    est = 2 * (tm * LANE * 4 + D_MODEL * 2 * D_FF * 2 + D_FF * D_MODEL * 2 + tm * D_MODEL * 4) \
        + 2 * tm * D_MODEL * 4 + tm * D_MODEL * 2 + 6 * tm * 2 * D_FF * 4
    grid_spec = pltpu.PrefetchScalarGridSpec(
        num_scalar_prefetch=2,
        grid=(MOE_BLOCKS,),
        in_specs=[pl.BlockSpec((None, 1, tm), lambda i, be, nu: (0, 0, 0), memory_space=pltpu.SMEM),
                  pl.BlockSpec((None, 1, tm), lambda i, be, nu: (jnp.minimum(i + 1, MOE_BLOCKS - 1), 0, 0),
                               memory_space=pltpu.SMEM),
                  pl.BlockSpec(memory_space=pl.ANY),
                  pl.BlockSpec((tm, 1), row_map),
                  pl.BlockSpec((None, D_MODEL, 2 * D_FF), exp_map),
                  pl.BlockSpec((None, 1, 2 * D_FF), exp_map),
                  pl.BlockSpec((None, D_FF, D_MODEL), exp_map),
                  pl.BlockSpec((None, 1, D_MODEL), exp_map)],
        out_specs=pl.BlockSpec((tm, D_MODEL), lambda i, be, nu: (i, 0)),
        scratch_shapes=[pltpu.VMEM((2, tm, D_MODEL), F32), pltpu.SemaphoreType.DMA((2,))],
    )
    return pl.pallas_call(
        _moe_kernel,
        out_shape=jax.ShapeDtypeStruct((MOE_SLOTS, D_MODEL), F32),
        grid_spec=grid_spec,
        compiler_params=_cparams(("arbitrary",), est),
        name="moe_experts",
    )(blk_e, n_used, slot_tok, slot_tok, x, slot_w, w_gu, b_gu, w_dn, b_dn)


def _rms_norm(x, g):
    return x * lax.rsqrt(jnp.mean(x * x, axis=-1, keepdims=True) + RMS_EPS) * g


def _layer_norm(x, g, b):
    xc = x - jnp.mean(x, axis=-1, keepdims=True)
    var = jnp.mean(xc * xc, axis=-1, keepdims=True)
    return xc * lax.rsqrt(var + LN_EPS) * g + b


def _rope(x, pos, rot_dim):
    half = rot_dim // 2
    freqs = ROPE_THETA ** (-jnp.arange(half, dtype=F32) * (2.0 / rot_dim))
    ang = pos.astype(F32)[:, None] * freqs[None, :]
    shape = (ang.shape[0],) + (1,) * (x.ndim - 2) + (half,)
    cos = jnp.cos(ang).reshape(shape)
    sin = jnp.sin(ang).reshape(shape)
    x1, x2, rest = x[..., :half], x[..., half:rot_dim], x[..., rot_dim:]
    return jnp.concatenate([x1 * cos - x2 * sin, x2 * cos + x1 * sin, rest], axis=-1)


def _pad_rows(x, n):
    return jnp.pad(x, ((0, 0), (0, n - x.shape[1]), (0, 0)))


def _select_blocks(imp, q_pos, nbs):
    nb = imp.shape[-1]
    imp = jnp.pad(imp, ((0, 0), (0, 0), (0, nbs - nb)), constant_values=-1.0)
    blk = jnp.arange(nbs)[None, :]
    cur = (q_pos // NSA_BLK)[:, None]
    forced = (blk == 0) | (blk == cur) | (blk == cur - 1)
    score = jnp.where(blk > cur, -1.0, jnp.where(forced, FORCE_SCORE, imp))
    n_sel = min(NSA_TOPN, nbs)
    top_s, top_i = lax.top_k(score, n_sel)
    onehot = (top_i[..., None] == jnp.arange(nbs)) & (top_s >= 0.0)[..., None]
    return jnp.any(onehot, axis=-2).astype(F32)


def _moe(x, router_w, router_b, w_gu, b_gu, w_dn, b_dn, layer):
    n_tok = x.shape[0]
    n_asg = n_tok * TOP_K
    top_v, top_e = _router(x, router_w, router_b, tm=512, name=f"router_l{layer}")
    gates = jax.nn.softmax(top_v, axis=-1)
    flat_e = top_e.reshape(-1)
    onehot = (flat_e[:, None] == jnp.arange(N_EXPERTS)[None, :]).astype(jnp.int32)
    running = jnp.cumsum(onehot, axis=0)
    rank = jnp.sum((running - 1) * onehot, axis=1)
    counts = running[-1]
    padded = (counts + MOE_TM - 1) // MOE_TM * MOE_TM
    pad_end = jnp.cumsum(padded)
    pad_start = pad_end - padded
    dest = (pad_start[flat_e] + rank).astype(jnp.int32)
    tok_of = (jnp.arange(n_asg) // TOP_K).astype(F32)
    slot_pair = jnp.zeros((MOE_SLOTS, 2), F32).at[dest].set(jnp.stack([tok_of, gates.reshape(-1)], axis=1))
    slot_tok = slot_pair[:, 0].astype(jnp.int32)
    slot_w = slot_pair[:, 1]
    blk_start = jnp.arange(MOE_BLOCKS) * MOE_TM
    blk_e = jnp.minimum(jnp.sum((pad_end[None, :] <= blk_start[:, None]).astype(jnp.int32), axis=1),
                        N_EXPERTS - 1).astype(jnp.int32)
    n_used = (pad_end[-1] // MOE_TM).astype(jnp.int32).reshape(1)
    ys = _moe_experts(blk_e, n_used, slot_tok.reshape(MOE_BLOCKS, 1, MOE_TM), x, slot_w[:, None],
                      w_gu, b_gu[:, None, :], w_dn, b_dn[:, None, :])
    slot_of = dest.reshape(n_tok, TOP_K)
    return [ys[slot_of[:, k]] for k in range(TOP_K)]


def _prep_weights(w_in, mla_w_uq, mla_w_uk, mla_w_uv, nsa_phi_w1, nsa_phi_w2):
    d = w_in.shape[0]
    z = lambda n: jnp.zeros((d, D_MODEL, n), F32)
    w_in_p = jnp.concatenate([w_in[:, :, 4176:], w_in[:, :, :672], z(96), w_in[:, :, 672:4176], z(80),
                              z(IN_PAD_N - IN_GN - LANE)], axis=2).astype(BF16)
    uq = mla_w_uq.reshape(d, MLA_Q_LORA, MLA_HEADS, MLA_NOPE + MLA_ROPE)
    w_uq_p = jnp.concatenate([uq[..., :MLA_NOPE].reshape(d, MLA_Q_LORA, -1),
                              uq[..., MLA_NOPE:].reshape(d, MLA_Q_LORA, -1)], axis=2).astype(BF16)
    w_uk_t = jnp.transpose(mla_w_uk, (0, 2, 3, 1)).astype(BF16)
    w_uv_t = jnp.transpose(mla_w_uv, (0, 2, 1, 3)).astype(BF16)
    w1 = nsa_phi_w1.reshape(d, 2, NSA_BLK, NSA_DH, NSA_PHI_HID)
    zero = jnp.zeros_like(w1[:, 0])
    w1k = jnp.concatenate([w1[:, 0], zero], axis=-1)
    w1v = jnp.concatenate([zero, w1[:, 1]], axis=-1)
    w1r = jnp.concatenate([w1k, w1v], axis=2)
    w1_flat = w1r.reshape(d, NSA_BLK * 2 * NSA_DH, 2 * NSA_PHI_HID).astype(BF16)
    w1_pair = w1r.reshape(d, NSA_BLK // 2, 4 * NSA_DH, 2 * NSA_PHI_HID).astype(BF16)
    z2 = jnp.zeros((d, NSA_PHI_HID, NSA_DH), F32)
    w2bd = jnp.concatenate([jnp.concatenate([nsa_phi_w2[:, 0], z2], axis=2),
                            jnp.concatenate([z2, nsa_phi_w2[:, 1]], axis=2)], axis=1).astype(BF16)
    return w_in_p, w_uq_p, w_uk_t, w_uv_t, w1_flat, w1_pair, w2bd


def kernel(x_prompt, x_sample, cache_mla, cache_nsa_cmp, cache_nsa_sel, state_nsa_win, state_conv, page_table,
           w_in, mla_q_norm, mla_kv_norm, mla_w_uq, mla_w_uk, mla_w_uv, mla_w_br,
           conv_w, conv_b, conv_ln_g, conv_ln_b, conv_w_br,
           nsa_phi_pe, nsa_phi_w1, nsa_phi_w2, nsa_w_br, w_out,
           ln1_g, ln1_b, ln2_g, ln2_b, router_w, router_b, moe_w_gu, moe_b_gu, moe_w_dn, moe_b_dn):
    w_in_p, w_uq_p, w_uk_t, w_uv_t, w1_flat, w1_pair, w2bd = _prep_weights(
        w_in, mla_w_uq, mla_w_uk, mla_w_uv, nsa_phi_w1, nsa_phi_w2)
    pos = jnp.concatenate([jnp.tile(jnp.arange(SEQ), BATCH),
                           jnp.tile(PAST_LEN + jnp.arange(DEC_SEQ), DEC_BATCH)])
    q_pos_s = PAST_LEN + jnp.arange(DEC_SEQ)
    nbs_s = -(-(PAST_LEN + DEC_SEQ) // NSA_BLK)
    mp = M_PROMPT
    x = jnp.concatenate([x_prompt.reshape(mp, D_MODEL), x_sample.reshape(M_SAMPLE, D_MODEL)], axis=0)
    cache_mla_t = jnp.swapaxes(cache_mla, 2, 3)
    st_p, st_s = [], []
    for l in range(DEPTH):
        h = _mm(x, w_in_p[l], tm=512, tn=1536, name=f"in_proj_l{l}")
        cq = _rms_norm(h[:, IN_CQ:IN_CQ + MLA_Q_LORA], mla_q_norm[l])
        q = _mm(cq, w_uq_p[l], tm=1088, tn=1536, name=f"mla_uq_l{l}")
        q_lat = _head_up(q[:, :MLA_HEADS * MLA_NOPE], w_uk_t[l], tm=1088, out_dtype=BF16, name=f"mla_uk_l{l}",
                         scale=MLA_SCALE)
        q_rope = _rope(q[:, MLA_HEADS * MLA_NOPE:].reshape(M_ALL, MLA_HEADS, MLA_ROPE), pos, MLA_ROPE)
        q_rope = jnp.transpose(q_rope * MLA_SCALE, (1, 0, 2)).astype(BF16)
        mla_rows = jnp.concatenate([_rms_norm(h[:, IN_CKV:IN_CKV + MLA_KV_LORA], mla_kv_norm[l]),
                                    _rope(h[:, IN_KR:IN_KR + MLA_ROPE], pos, MLA_ROPE)], axis=1)
        mla_rows_p = mla_rows[:mp].reshape(BATCH, SEQ, MLA_CACHE)
        mla_rows_s = mla_rows[mp:].reshape(DEC_BATCH, DEC_SEQ, MLA_CACHE)
        o_lat_p = _mla_prompt(q_lat[:, :mp].reshape(MLA_HEADS, BATCH, SEQ, MLA_KV_LORA),
                              q_rope[:, :mp].reshape(MLA_HEADS, BATCH, SEQ, MLA_ROPE), mla_rows_p)

        def to_rows(a):
            dd = a.shape[-1]
            return jnp.transpose(a.reshape(MLA_HEADS, DEC_BATCH, DEC_SEQ, dd), (1, 2, 0, 3)).reshape(
                DEC_BATCH, DEC_SEQ * MLA_HEADS, dd)

        o_lat_s = _mla_sample(page_table, to_rows(q_lat[:, mp:]), to_rows(q_rope[:, mp:]),
                              jnp.swapaxes(_pad_rows(mla_rows_s, PAGE_SIZE), 1, 2), cache_mla_t, l)
        o_lat_s = jnp.transpose(o_lat_s.reshape(DEC_BATCH, DEC_SEQ, MLA_HEADS, MLA_KV_LORA),
                                (2, 0, 1, 3)).reshape(MLA_HEADS, M_SAMPLE, MLA_KV_LORA)
        o_lat = jnp.concatenate([o_lat_p.reshape(MLA_HEADS, mp, MLA_KV_LORA), o_lat_s], axis=1)
        o_mla = _head_down(o_lat, w_uv_t[l], tm=1088, out_dtype=BF16, name=f"mla_uv_l{l}")
        y_a = _mm(o_mla, mla_w_br[l].astype(BF16), tm=1088, tn=1024, name=f"mla_br_l{l}")
        glu = h[:, IN_GLU:IN_GLU + 2 * CONV_DIM]
        u = glu[:, :CONV_DIM] * jax.nn.sigmoid(glu[:, CONV_DIM:])
        u_p = u[:mp].reshape(BATCH, SEQ, CONV_DIM)
        u_s = u[mp:].reshape(DEC_BATCH, DEC_SEQ, CONV_DIM)
        c_p = _conv_prompt(u_p, conv_w[l], conv_b[l], conv_ln_g[l], conv_ln_b[l])
        u_ext = jnp.concatenate([state_conv[:, l], u_s], axis=1)
        y_s = sum(u_ext[:, k:k + DEC_SEQ] * conv_w[l][k] for k in range(CONV_WIDTH)) + conv_b[l]
        c_s = jax.nn.silu(_layer_norm(y_s, conv_ln_g[l], conv_ln_b[l])).astype(BF16)
        c_all = jnp.concatenate([c_p.reshape(mp, CONV_DIM), c_s.reshape(M_SAMPLE, CONV_DIM)], axis=0)
        y_conv = _mm(c_all, conv_w_br[l].astype(BF16), tm=1088, tn=1024, name=f"conv_br_l{l}")
        qn = h[:, IN_QN:IN_QN + NSA_HEADS * NSA_DH]
        qr = _rope(qn.reshape(M_ALL, NSA_HEADS, NSA_DH), pos, NSA_ROT) * NSA_SCALE
        kv_cmp = h[:, IN_CMP:IN_CMP + 2 * NSA_DH]
        kv_sel = h[:, IN_SEL:IN_SEL + 2 * NSA_DH]
        kv_sel = jnp.concatenate([_rope(kv_sel[:, :NSA_DH], pos, NSA_ROT), kv_sel[:, NSA_DH:]], axis=1)
        kv_win = h[:, IN_WIN:IN_WIN + 2 * NSA_DH]
        kv_win = jnp.concatenate([_rope(kv_win[:, :NSA_DH], pos, NSA_ROT), kv_win[:, NSA_DH:]], axis=1)
        g_nsa = jax.nn.sigmoid(h[:, IN_GN:IN_GN + 3 * NSA_HEADS]).reshape(M_ALL, NSA_HEADS, 3)
        pe2 = nsa_phi_pe[l].reshape(NSA_BLK, 2 * NSA_DH)
        kv_cmp_p = kv_cmp[:mp].reshape(BATCH, SEQ, 2 * NSA_DH)
        nb_p = SEQ // NSA_BLK
        flat = (kv_cmp_p.reshape(BATCH, nb_p, NSA_BLK, 2 * NSA_DH) + pe2).reshape(BATCH * nb_p, -1)
        hid = jax.nn.gelu(_mm(flat, w1_flat[l], tm=BATCH * nb_p, tn=2 * NSA_PHI_HID, name=f"nsa_phi1_l{l}"))
        kcvc_p = _mm(hid, w2bd[l], tm=BATCH * nb_p, tn=2 * NSA_DH, name=f"nsa_phi2_l{l}")
        o_c_p, selmask_p = _nsa_cmp_prompt(qn[:mp].reshape(BATCH, SEQ, -1),
                                           kcvc_p.reshape(BATCH, nb_p, 2 * NSA_DH))
        qr_p = qr[:mp].reshape(BATCH, SEQ, NSA_HEADS * NSA_DH)
        kv_sel_p = kv_sel[:mp].reshape(BATCH, SEQ, 2 * NSA_DH)
        kv_win_p = kv_win[:mp].reshape(BATCH, SEQ, 2 * NSA_DH)
        o_s_p = _nsa_sel_prompt(qr_p, kv_sel_p, selmask_p)
        o_w_p = _nsa_win_prompt(qr_p, kv_win_p)
        to_tok = lambda a: a.reshape(mp, NSA_HEADS, NSA_DH)
        o_c_p, o_s_p, o_w_p = to_tok(o_c_p), to_tok(o_s_p), to_tok(o_w_p)
        kv_sel_s = kv_sel[mp:].reshape(DEC_BATCH, DEC_SEQ, 2 * NSA_DH)
        kv_win_s = kv_win[mp:].reshape(DEC_BATCH, DEC_SEQ, 2 * NSA_DH)
        qn_s = qn[mp:].reshape(DEC_BATCH, DEC_SEQ * NSA_HEADS, NSA_DH).astype(BF16)
        qr_s = qr[mp:].reshape(DEC_BATCH, DEC_SEQ * NSA_HEADS, NSA_DH).astype(BF16)
        o_c_s, imp_s = _nsa_cmp_sample(page_table, qn_s, pe2, w1_pair[l], w2bd[l], cache_nsa_cmp, l)
        selmask_s = _select_blocks(imp_s, q_pos_s, nbs_s)
        key_pos = jnp.arange(PAST_LEN + PAGE_SIZE)
        allowed = jnp.repeat(selmask_s[:, :, :nbs_s - 1], NSA_BLK, axis=2)
        new_ok = selmask_s[:, :, nbs_s - 1:] * (key_pos[None, None, PAST_LEN:] <= q_pos_s[None, :, None])
        allowed = jnp.concatenate([allowed, new_ok.astype(F32)], axis=2)
        o_s_s, o_w_s = _nsa_selwin_sample(page_table, qr_s, allowed, _pad_rows(kv_sel_s, PAGE_SIZE),
                                          state_nsa_win, _pad_rows(kv_win_s, PAGE_SIZE), cache_nsa_sel, l)
        cat = lambda a, b: jnp.concatenate([a, b.reshape(M_SAMPLE, NSA_HEADS, NSA_DH)], axis=0)
        o_nsa = (g_nsa[..., 0:1] * cat(o_c_p, o_c_s) + g_nsa[..., 1:2] * cat(o_s_p, o_s_s)
                 + g_nsa[..., 2:3] * cat(o_w_p, o_w_s)).reshape(M_ALL, NSA_HEADS * NSA_DH)
        y_n = _mm(o_nsa, nsa_w_br[l].astype(BF16), tm=1088, tn=1024, name=f"nsa_br_l{l}")
        x1 = _mix_out(y_a, y_conv, y_n, h, x, w_out[l].astype(BF16), ln1_g[l], ln1_b[l], tm=256,
                      name=f"mix_out_l{l}")
        f_parts = _moe(x1, router_w[l], router_b[l], moe_w_gu[l].astype(BF16), moe_b_gu[l],
                       moe_w_dn[l].astype(BF16), moe_b_dn[l], l)
        x = _moe_combine(x1, f_parts, ln2_g[l], ln2_b[l], tm=256, name=f"moe_combine_l{l}")
        win_all = jnp.concatenate([state_nsa_win[:, l], kv_win_s], axis=1)
        st_p.append((mla_rows_p, kv_cmp_p, kv_sel_p, kv_win_p[:, -min(NSA_WINDOW, SEQ):],
                     u_p[:, -(CONV_WIDTH - 1):]))
        st_s.append((mla_rows_s, kv_cmp[mp:].reshape(DEC_BATCH, DEC_SEQ, 2 * NSA_DH), kv_sel_s,
                     win_all[:, -state_nsa_win.shape[2]:], u_ext[:, -(CONV_WIDTH - 1):]))
    stack = lambda sts, i: jnp.stack([s[i] for s in sts], axis=1)
    return (x[:mp].reshape(BATCH, SEQ, D_MODEL), x[mp:].reshape(DEC_BATCH, DEC_SEQ, D_MODEL),
            stack(st_p, 0), stack(st_p, 1), stack(st_p, 2), stack(st_p, 3), stack(st_p, 4),
            stack(st_s, 0), stack(st_s, 1), stack(st_s, 2), stack(st_s, 3), stack(st_s, 4))
```
